```python
import jax, jax.numpy as jnp
from jax import lax
import numpy as np

D_MODEL = 1024
BATCH = 16
SEQ = 4096
DEPTH = 2
DEC_BATCH = 32
DEC_SEQ = 64
PAST_LEN = 4096

CHUNK = 64
N_MIXERS = 2
N_POOL_LAYERS = (DEPTH + 1) // 2
N_GDN_LAYERS = DEPTH // 2
EPS = 1e-6

POOL_EXPAND = 2
POOL_WIDTH = POOL_EXPAND * D_MODEL
POOL_WINDOWS = (2, 4, 8, 16)
N_POOL_GROUPS = len(POOL_WINDOWS)
POOL_GROUP_DIM = POOL_WIDTH // N_POOL_GROUPS
POOL_BUF = max(POOL_WINDOWS) - 1

GDN_HEAD_DIM = 128
GDN_QK_HEADS = D_MODEL // GDN_HEAD_DIM
GDN_V_HEADS = 2 * GDN_QK_HEADS
GDN_KEY_DIM = GDN_QK_HEADS * GDN_HEAD_DIM
GDN_VALUE_DIM = GDN_V_HEADS * GDN_HEAD_DIM
GDN_CONV_DIM = 2 * GDN_KEY_DIM + GDN_VALUE_DIM
GDN_CONV_WIDTH = 4
GDN_IN_DIM = GDN_CONV_DIM + GDN_VALUE_DIM + 2 * GDN_V_HEADS

kernel_name = 'hybrid_pool_gdn_streaming_step'


def _rmsnorm(x, gain):
    xf = x.astype(jnp.float32)
    r = lax.rsqrt(jnp.mean(xf * xf, axis=-1, keepdims=True) + EPS)
    return xf * r * gain.astype(jnp.float32)


def _l2norm(x):
    xf = x.astype(jnp.float32)
    return xf * lax.rsqrt(jnp.sum(xf * xf, axis=-1, keepdims=True) + EPS)


def _pool_mixer(h, buf, start, w_in, w_group, scale, w_out):
    bsz, seqlen, _ = h.shape
    proj = h @ w_in
    u, z = proj[..., :POOL_WIDTH], proj[..., POOL_WIDTH:]
    ext = jnp.concatenate([buf.astype(u.dtype), u], axis=1)
    cs = jnp.cumsum(ext.astype(jnp.float32), axis=1)
    cs = jnp.pad(cs, ((0, 0), (1, 0), (0, 0)))
    pos = (start + jnp.arange(seqlen)).astype(jnp.float32)
    p = POOL_BUF
    pooled = []
    for gi, w in enumerate(POOL_WINDOWS):
        lo, hi = gi * POOL_GROUP_DIM, (gi + 1) * POOL_GROUP_DIM
        wsum = cs[:, p + 1:p + 1 + seqlen, lo:hi] - cs[:, p + 1 - w:p + 1 - w + seqlen, lo:hi]
        cnt = jnp.minimum(pos + 1.0, float(w))
        pooled.append(wsum / cnt[None, :, None])
    pooled = jnp.concatenate(pooled, axis=-1)
    d = (pooled - u.astype(jnp.float32)).reshape(bsz, seqlen, N_POOL_GROUPS, POOL_GROUP_DIM)
    y = jnp.einsum('blgc,gcd->blgd', d, w_group.astype(jnp.float32)).reshape(bsz, seqlen, POOL_WIDTH)
    y = y * scale.astype(jnp.float32) * jax.nn.silu(z.astype(jnp.float32))
    out = y.astype(h.dtype) @ w_out
    return out, ext[:, -POOL_BUF:]


def _chunked_gated_delta(q, k, v, g, beta, s0):
    bsz, seqlen, nh, dk = q.shape
    dv = v.shape[-1]
    c = min(CHUNK, seqlen)
    n = seqlen // c
    f32 = jnp.float32

    def blk(t):
        return t.astype(f32).reshape(bsz, n, c, nh, t.shape[-1]).transpose(0, 3, 1, 2, 4)

    qb, kb_, vb = blk(q), blk(k), blk(v)
    gb = g.astype(f32).reshape(bsz, n, c, nh).transpose(0, 3, 1, 2)
    bb = beta.astype(f32).reshape(bsz, n, c, nh).transpose(0, 3, 1, 2)
    gc = jnp.cumsum(gb, axis=-1)
    idx = jnp.arange(c)
    incl = idx[:, None] >= idx[None, :]
    strict = idx[:, None] > idx[None, :]
    diff = gc[..., :, None] - gc[..., None, :]
    decay = jnp.where(incl, jnp.exp(jnp.where(incl, diff, 0.0)), 0.0)
    kbeta = kb_ * bb[..., None]
    lmat = jnp.where(strict, jnp.einsum('bhncd,bhnjd->bhncj', kbeta, kb_) * decay, 0.0)
    rhs = jnp.concatenate([vb * bb[..., None], kbeta * jnp.exp(gc)[..., None]], axis=-1)
    sol = lax.linalg.triangular_solve(lmat, rhs, left_side=True, lower=True, unit_diagonal=True)
    value, kcd = sol[..., :dv], sol[..., dv:]
    aqk = jnp.einsum('bhncd,bhnjd->bhncj', qb, kb_) * decay
    qdec = qb * jnp.exp(gc)[..., None]
    glast = gc[..., -1]
    kdec = kb_ * jnp.exp(glast[..., None] - gc)[..., None]
    xs = tuple(jnp.moveaxis(t, 2, 0) for t in (value, kcd, aqk, qdec, kdec, glast))

    def step(s, inp):
        val, kc, a, qd, kd, gl = inp
        v_new = val - jnp.einsum('bhcd,bhde->bhce', kc, s)
        o = jnp.einsum('bhcd,bhde->bhce', qd, s) + jnp.einsum('bhcj,bhje->bhce', a, v_new)
        s = s * jnp.exp(gl)[..., None, None] + jnp.einsum('bhcd,bhce->bhde', kd, v_new)
        return s, o

    s_fin, o = lax.scan(step, s0.astype(f32), xs)
    o = o.transpose(1, 0, 3, 2, 4).reshape(bsz, seqlen, nh, dv)
    return o, s_fin


def _gdn_mixer(h, conv_buf, s0, w_in, conv_w, a_log, dt_bias, norm_w, w_out):
    bsz, seqlen, _ = h.shape
    proj = h @ w_in
    o1 = GDN_CONV_DIM
    o2 = o1 + GDN_VALUE_DIM
    o3 = o2 + GDN_V_HEADS
    qkv, z, a, b = proj[..., :o1], proj[..., o1:o2], proj[..., o2:o3], proj[..., o3:]
    ext = jnp.concatenate([conv_buf.astype(qkv.dtype), qkv], axis=1)
    conv = ext[:, 0:seqlen] * conv_w[0]
    for t in range(1, GDN_CONV_WIDTH):
        conv = conv + ext[:, t:t + seqlen] * conv_w[t]
    conv = jax.nn.silu(conv)
    q = conv[..., :GDN_KEY_DIM].reshape(bsz, seqlen, GDN_QK_HEADS, GDN_HEAD_DIM)
    k = conv[..., GDN_KEY_DIM:2 * GDN_KEY_DIM].reshape(bsz, seqlen, GDN_QK_HEADS, GDN_HEAD_DIM)
    v = conv[..., 2 * GDN_KEY_DIM:].reshape(bsz, seqlen, GDN_V_HEADS, GDN_HEAD_DIM)
    rep = GDN_V_HEADS // GDN_QK_HEADS
    q = jnp.repeat(_l2norm(q) * GDN_HEAD_DIM ** -0.5, rep, axis=2)
    k = jnp.repeat(_l2norm(k), rep, axis=2)
    beta = jax.nn.sigmoid(b.astype(jnp.float32))
    g = -jnp.exp(a_log.astype(jnp.float32)) * jax.nn.softplus(a.astype(jnp.float32) + dt_bias.astype(jnp.float32))
    o, s_new = _chunked_gated_delta(q, k, v, g, beta, s0)
    zg = jax.nn.silu(z.astype(jnp.float32)).reshape(bsz, seqlen, GDN_V_HEADS, GDN_HEAD_DIM)
    o = _rmsnorm(o, norm_w) * zg
    out = o.reshape(bsz, seqlen, GDN_VALUE_DIM).astype(h.dtype) @ w_out
    return out, ext[:, -(GDN_CONV_WIDTH - 1):], s_new


def _trunk(x, c, pool_bufs, conv_bufs, rec_states, start,
           norm_gain, ada_w, ada_b, pool_w_in, pool_w_group, pool_scale, pool_w_out,
           gdn_w_in, gdn_conv_w, gdn_a_log, gdn_dt_bias, gdn_norm_w, gdn_w_out, final_gain):
    new_pool, new_conv, new_rec = [], [], []
    for i in range(DEPTH):
        mod = jax.nn.silu(c) @ ada_w[i] + ada_b[i]
        shift, scale, gate = jnp.split(mod, 3, axis=-1)
        h = (_rmsnorm(x, norm_gain[i]) * (1.0 + scale[:, None, :]) + shift[:, None, :]).astype(x.dtype)
        j = i // N_MIXERS
        if i % N_MIXERS == 0:
            out, buf = _pool_mixer(h, pool_bufs[j], start, pool_w_in[j], pool_w_group[j],
                                   pool_scale[j], pool_w_out[j])
            new_pool.append(buf)
        else:
            out, cbuf, s_new = _gdn_mixer(h, conv_bufs[j], rec_states[j], gdn_w_in[j], gdn_conv_w[j],
                                          gdn_a_log[j], gdn_dt_bias[j], gdn_norm_w[j], gdn_w_out[j])
            new_conv.append(cbuf)
            new_rec.append(s_new)
        x = x + gate[:, None, :] * out
    y = _rmsnorm(x, final_gain).astype(x.dtype)
    return y, jnp.stack(new_pool), jnp.stack(new_conv), jnp.stack(new_rec)


def setup_inputs(seed: int = 0) -> dict:
    key = jax.random.key(seed)
    ks = jax.random.split(key, 24)
    f32 = jnp.float32

    def nrm(k, shape, s):
        return jax.random.normal(k, shape, f32) * s

    dt = jnp.exp(jax.random.uniform(ks[17], (N_GDN_LAYERS, GDN_V_HEADS), f32, np.log(0.001), np.log(0.1)))
    return {
        'x_prompt': nrm(ks[0], (BATCH, SEQ, D_MODEL), 1.0),
        'x_sample': nrm(ks[1], (DEC_BATCH, DEC_SEQ, D_MODEL), 1.0),
        'c_prompt': nrm(ks[2], (BATCH, D_MODEL), 1.0),
        'c_sample': nrm(ks[3], (DEC_BATCH, D_MODEL), 1.0),
        'state_pool': nrm(ks[4], (N_POOL_LAYERS, DEC_BATCH, POOL_BUF, POOL_WIDTH), 1.0),
        'state_conv': nrm(ks[5], (N_GDN_LAYERS, DEC_BATCH, GDN_CONV_WIDTH - 1, GDN_CONV_DIM), 1.0),
        'state_rec': nrm(ks[6], (N_GDN_LAYERS, DEC_BATCH, GDN_V_HEADS, GDN_HEAD_DIM, GDN_HEAD_DIM), GDN_HEAD_DIM ** -0.5),
        'norm_gain': 1.0 + nrm(ks[7], (DEPTH, D_MODEL), 0.05),
        'ada_w': nrm(ks[8], (DEPTH, D_MODEL, 3 * D_MODEL), 0.5 * D_MODEL ** -0.5),
        'ada_b': nrm(ks[9], (DEPTH, 3 * D_MODEL), 0.02),
        'pool_w_in': nrm(ks[10], (N_POOL_LAYERS, D_MODEL, 2 * POOL_WIDTH), D_MODEL ** -0.5),
        'pool_w_group': nrm(ks[11], (N_POOL_LAYERS, N_POOL_GROUPS, POOL_GROUP_DIM, POOL_GROUP_DIM), POOL_GROUP_DIM ** -0.5),
        'pool_scale': 1.0 + nrm(ks[12], (N_POOL_LAYERS, POOL_WIDTH), 0.1),
        'pool_w_out': nrm(ks[13], (N_POOL_LAYERS, POOL_WIDTH, D_MODEL), POOL_WIDTH ** -0.5),
        'gdn_w_in': nrm(ks[14], (N_GDN_LAYERS, D_MODEL, GDN_IN_DIM), D_MODEL ** -0.5),
        'gdn_conv_w': nrm(ks[15], (N_GDN_LAYERS, GDN_CONV_WIDTH, GDN_CONV_DIM), 0.5),
        'gdn_a_log': jnp.log(jax.random.uniform(ks[16], (N_GDN_LAYERS, GDN_V_HEADS), f32, 1.0, 16.0)),
        'gdn_dt_bias': jnp.log(jnp.expm1(dt)),
        'gdn_norm_w': 1.0 + nrm(ks[18], (N_GDN_LAYERS, GDN_HEAD_DIM), 0.05),
        'gdn_w_out': nrm(ks[19], (N_GDN_LAYERS, GDN_VALUE_DIM, D_MODEL), GDN_VALUE_DIM ** -0.5),
        'final_gain': 1.0 + nrm(ks[20], (D_MODEL,), 0.05),
    }


def reference(x_prompt, x_sample, c_prompt, c_sample, state_pool, state_conv, state_rec,
              norm_gain, ada_w, ada_b, pool_w_in, pool_w_group, pool_scale, pool_w_out,
              gdn_w_in, gdn_conv_w, gdn_a_log, gdn_dt_bias, gdn_norm_w, gdn_w_out, final_gain):
    nb = x_prompt.shape[0]
    zero_pool = jnp.zeros((N_POOL_LAYERS, nb, POOL_BUF, POOL_WIDTH), x_prompt.dtype)
    zero_conv = jnp.zeros((N_GDN_LAYERS, nb, GDN_CONV_WIDTH - 1, GDN_CONV_DIM), x_prompt.dtype)
    zero_rec = jnp.zeros((N_GDN_LAYERS, nb, GDN_V_HEADS, GDN_HEAD_DIM, GDN_HEAD_DIM), jnp.float32)
    y_prompt, pool_p, conv_p, rec_p = _trunk(
        x_prompt, c_prompt, zero_pool, zero_conv, zero_rec, 0,
        norm_gain, ada_w, ada_b, pool_w_in, pool_w_group, pool_scale, pool_w_out,
        gdn_w_in, gdn_conv_w, gdn_a_log, gdn_dt_bias, gdn_norm_w, gdn_w_out, final_gain)
    y_sample, pool_s, conv_s, rec_s = _trunk(
        x_sample, c_sample, state_pool, state_conv, state_rec, PAST_LEN,
        norm_gain, ada_w, ada_b, pool_w_in, pool_w_group, pool_scale, pool_w_out,
        gdn_w_in, gdn_conv_w, gdn_a_log, gdn_dt_bias, gdn_norm_w, gdn_w_out, final_gain)
    return (y_prompt, y_sample, pool_p, conv_p, rec_p, pool_s, conv_s, rec_s)
```

```python
import functools

import jax
import jax.numpy as jnp
from jax import lax
from jax.experimental import pallas as pl
from jax.experimental.pallas import tpu as pltpu

F32 = jnp.float32
BF16 = jnp.bfloat16
EPS = 1e-6

POOL_WINDOWS = (2, 4, 8, 16)
POOL_CARRY = 16
CONV_WIDTH = 4
CONV_CARRY = 8
HEAD_DIM = 128
CHUNK = 64
LANES = 128
VMEM_LIMIT_BYTES = 58 * 1024 * 1024
PAST_LEN = 4096


def _dot(a, b):
    return jnp.dot(a, b, preferred_element_type=F32)


def _dot_nt(a, b):
    return lax.dot_general(a, b, (((1,), (1,)), ((), ())), preferred_element_type=F32)


def _split2(a):
    hi = a.astype(BF16)
    lo = (a - hi.astype(F32)).astype(BF16)
    return hi, lo


def _split3(a):
    p1 = a.astype(BF16)
    r1 = a - p1.astype(F32)
    p2 = r1.astype(BF16)
    p3 = (r1 - p2.astype(F32)).astype(BF16)
    return p1, p2, p3


def _mm3(a, b):
    ah, al = _split2(a)
    bh, bl = _split2(b)
    return _dot(ah, bh) + (_dot(ah, bl) + _dot(al, bh))


def _silu(x):
    return x / (1.0 + jnp.exp(-x))


def _softplus(x):
    return jnp.maximum(x, 0.0) + jnp.log(1.0 + jnp.exp(-jnp.abs(x)))


def _const_spec(shape):
    nd = len(shape)
    return pl.BlockSpec(shape, lambda *_: (0,) * nd, pipeline_mode=pl.Buffered(1))


def _mod_kernel(c_ref, w_ref, b_ref, o_ref):
    a = _silu(c_ref[...])
    o_ref[0, 0] = _mm3(a, w_ref[0]) + b_ref[0]


def _adaln_mod(c_all, ada_w, ada_b):
    depth, d, d3 = ada_w.shape
    nb = c_all.shape[0]
    nt = d3 // d
    return pl.pallas_call(
        _mod_kernel,
        grid=(depth, nt),
        in_specs=[
            pl.BlockSpec((nb, d), lambda i, n: (0, 0)),
            pl.BlockSpec((1, d, d), lambda i, n: (i, 0, n)),
            pl.BlockSpec((1, 1, d), lambda i, n: (i, 0, n)),
        ],
        out_specs=pl.BlockSpec((1, 1, nb, d), lambda i, n: (i, n, 0, 0)),
        out_shape=jax.ShapeDtypeStruct((depth, nt, nb, d), F32),
        name="adaln_mod",
    )(c_all, ada_w, ada_b.reshape(depth, 1, d3))


def _modulated_norm(x, m, gain):
    r = lax.rsqrt(jnp.mean(x * x, axis=-1, keepdims=True) + EPS)
    return x * r * gain * (1.0 + m[1:2]) + m[0:1]


def _pool_kernel(x_ref, mod_ref, buf_ref, gain_ref, win_ref, wg_ref, sc_ref, wout_ref,
                 xo_ref, bufo_ref, carry_ref, *, tile, start):
    l = pl.program_id(1)
    width = sc_ref.shape[1]
    gdim = width // len(POOL_WINDOWS)

    @pl.when(l == 0)
    def _():
        carry_ref[...] = buf_ref[0]

    x = x_ref[0]
    m = mod_ref[0]
    h = _modulated_norm(x, m, gain_ref[...]).astype(BF16)
    pos = (start + l * tile + lax.broadcasted_iota(jnp.int32, (tile, 1), 0)).astype(F32)

    acc = jnp.zeros(x.shape, F32)
    for gi, w in enumerate(POOL_WINDOWS):
        lo = gi * gdim
        u = _dot(h, win_ref[:, lo:lo + gdim])
        z = _dot(h, win_ref[:, width + lo:width + lo + gdim])
        ext = jnp.concatenate([carry_ref[:, lo:lo + gdim], u], axis=0)
        s = ext
        for k in range(gi + 1):
            s = s + pltpu.roll(s, 1 << k, 0)
        inv_cnt = 1.0 / jnp.minimum(pos + 1.0, float(w))
        d = s[POOL_CARRY:] * inv_cnt - u
        y = _dot(d.astype(BF16), wg_ref[gi])
        y = y * sc_ref[:, lo:lo + gdim] * _silu(z)
        acc = acc + _dot(y.astype(BF16), wout_ref[lo:lo + gdim, :])
        carry_ref[:, lo:lo + gdim] = ext[tile:tile + POOL_CARRY]

    xo_ref[0] = x + m[2:3] * acc

    @pl.when(l == pl.num_programs(1) - 1)
    def _():
        bufo_ref[0] = carry_ref[...]


def _pool_layer(x, mod, buf, gain, w_in, w_group, scale, w_out, *, start, tile):
    nb, seqlen, d = x.shape
    width = scale.shape[-1]
    ng, gdim, _ = w_group.shape
    tile = tile if seqlen % tile == 0 else seqlen
    nl = seqlen // tile
    kern = functools.partial(_pool_kernel, tile=tile, start=start)
    return pl.pallas_call(
        kern,
        grid=(nb, nl),
        in_specs=[
            pl.BlockSpec((1, tile, d), lambda b, l: (b, l, 0)),
            pl.BlockSpec((1, 3, d), lambda b, l: (b, 0, 0)),
            pl.BlockSpec((1, POOL_CARRY, width), lambda b, l: (b, 0, 0)),
            _const_spec((1, d)),
            _const_spec((d, 2 * width)),
            _const_spec((ng, gdim, gdim)),
            _const_spec((1, width)),
            _const_spec((width, d)),
        ],
        out_specs=[
            pl.BlockSpec((1, tile, d), lambda b, l: (b, l, 0)),
            pl.BlockSpec((1, POOL_CARRY, width), lambda b, l: (b, 0, 0)),
        ],
        out_shape=[
            jax.ShapeDtypeStruct((nb, seqlen, d), F32),
            jax.ShapeDtypeStruct((nb, POOL_CARRY, width), F32),
        ],
        scratch_shapes=[pltpu.VMEM((POOL_CARRY, width), F32)],
        compiler_params=pltpu.CompilerParams(
            dimension_semantics=("parallel", "arbitrary"), vmem_limit_bytes=VMEM_LIMIT_BYTES),
        name="pool_layer",
    )(x, mod, buf, gain, w_in, w_group, scale, w_out)


def _gdn_kernel(x_ref, mod_ref, cbuf_ref, s0_ref, gain_ref, wqkv_ref, wz_ref, wab_ref, convw_ref,
                alog_ref, dtb_ref, normw_ref, wout_ref, fgain_ref,
                y_ref, cbo_ref, so_ref,
                cc_ref, s_ref, qs_ref, ks_ref, vs_ref, zs_ref, gc_ref, bb_ref, gr_ref,
                val_ref, kq_ref, aq_ref, kdt_ref, eg_ref, ob_ref, *, tile):
    l = pl.program_id(1)
    nqk = qs_ref.shape[0]
    nv = vs_ref.shape[0]
    nc = tile // CHUNK
    pair = 2 * CHUNK

    @pl.when(l == 0)
    def _():
        cc_ref[...] = cbuf_ref[0]
        s_ref[...] = s0_ref[0]

    x = x_ref[0]
    m = mod_ref[0]
    hf = _modulated_norm(x, m, gain_ref[...])
    h, h_lo = _split2(hf)

    nblk = wqkv_ref.shape[1] // (2 * HEAD_DIM)
    for blk in range(nblk):
        cs = slice(blk * 2 * HEAD_DIM, (blk + 1) * 2 * HEAD_DIM)
        pre = _dot(h, wqkv_ref[:, cs])
        ext = jnp.concatenate([cc_ref[:, cs], pre], axis=0)
        cw = convw_ref[:, cs]
        acc = ext * cw[CONV_WIDTH - 1:CONV_WIDTH]
        for t in range(CONV_WIDTH - 1):
            acc = acc + pltpu.roll(ext, CONV_WIDTH - 1 - t, 0) * cw[t:t + 1]
        conv = _silu(acc[CONV_CARRY:])
        cc_ref[:, cs] = ext[tile:tile + CONV_CARRY]
        for e in range(2):
            head = 2 * blk + e
            part = conv[:, e * HEAD_DIM:(e + 1) * HEAD_DIM]
            if head < 2 * nqk:
                part = part * lax.rsqrt(jnp.sum(part * part, axis=-1, keepdims=True) + EPS)
                if head < nqk:
                    qs_ref[head] = part * (HEAD_DIM ** -0.5)
                else:
                    ks_ref[head - nqk] = part
            else:
                vs_ref[head - 2 * nqk] = part

    for blk in range(nv // 2):
        zz = _silu(_dot(h, wz_ref[:, blk * 2 * HEAD_DIM:(blk + 1) * 2 * HEAD_DIM]))
        zs_ref[2 * blk] = zz[:, :HEAD_DIM]
        zs_ref[2 * blk + 1] = zz[:, HEAD_DIM:]

    wab_hi, wab_lo = _split2(wab_ref[...])
    ab = _dot(h, wab_hi) + (_dot(h, wab_lo) + _dot(h_lo, wab_hi))
    bb_ref[...] = 1.0 / (1.0 + jnp.exp(-ab))
    g = -jnp.exp(alog_ref[...]) * _softplus(ab + dtb_ref[...])
    ri = lax.broadcasted_iota(jnp.int32, (tile, tile), 0)
    ci = lax.broadcasted_iota(jnp.int32, (tile, tile), 1)
    tril = ((ri >= ci) & ((ri // CHUNK) == (ci // CHUNK))).astype(BF16)
    g1, g2, g3 = _split3(g)
    gc = _dot(tril, g1) + (_dot(tril, g2) + _dot(tril, g3))
    gc_ref[...] = gc
    er = lax.broadcasted_iota(jnp.int32, (LANES, LANES), 0)
    ec = lax.broadcasted_iota(jnp.int32, (LANES, LANES), 1)
    eye = (er == ec).astype(BF16)
    c1, c2, c3 = _split3(gc)
    gct = _dot_nt(eye, c1) + (_dot_nt(eye, c2) + _dot_nt(eye, c3))
    for c in range(nc):
        gr_ref[c * nqk:(c + 1) * nqk, 0:CHUNK] = gct[0:nqk, c * CHUNK:(c + 1) * CHUNK]
        gr_ref[c * nqk:(c + 1) * nqk, CHUNK:pair] = gct[nqk:nv, c * CHUNK:(c + 1) * CHUNK]

    pr = lax.broadcasted_iota(jnp.int32, (pair, pair), 0)
    pc = lax.broadcasted_iota(jnp.int32, (pair, pair), 1)
    same = (pr // CHUNK) == (pc // CHUNK)
    incl = same & (pr >= pc)
    strict = same & (pr > pc)
    ident = (pr == pc).astype(F32)
    top = lax.broadcasted_iota(jnp.int32, (pair, HEAD_DIM), 0) < CHUNK

    def phase_a(c, carry):
        rows = pl.ds(pl.multiple_of(c * CHUNK, CHUNK), CHUNK)
        for j in range(nqk):
            k = ks_ref[j, rows, :]
            q = qs_ref[j, rows, :]
            kst = jnp.concatenate([k, k], axis=0)
            qst = jnp.concatenate([q, q], axis=0)
            vst = jnp.concatenate([vs_ref[2 * j, rows, :], vs_ref[2 * j + 1, rows, :]], axis=0)
            kb = kst.astype(BF16)
            kk = _dot_nt(kb, kb)
            qk = _dot_nt(qst.astype(BF16), kb)
            gcol = jnp.concatenate([gc_ref[rows, j:j + 1], gc_ref[rows, nqk + j:nqk + j + 1]], axis=0)
            bcol = jnp.concatenate([bb_ref[rows, nv + j:nv + j + 1],
                                    bb_ref[rows, nv + nqk + j:nv + nqk + j + 1]], axis=0)
            grow = gr_ref[pl.ds(c * nqk + j, 1), :]
            decay = jnp.where(incl, jnp.exp(jnp.where(incl, gcol - grow, 0.0)), 0.0)
            nmat = jnp.where(strict, -(bcol * kk * decay), 0.0)
            xinv = ident + nmat
            p = nmat
            for _ in range(5):
                p = _mm3(p, p)
                xinv = xinv + _mm3(xinv, p)
            eg = jnp.exp(gcol)
            rhs = jnp.concatenate([vst * bcol, kst * (bcol * eg)], axis=1)
            sol = _mm3(xinv, rhs)
            value = sol[:, :HEAD_DIM]
            kcd = sol[:, HEAD_DIM:]
            qdec = qst * eg
            glast = jnp.concatenate([jnp.broadcast_to(gcol[CHUNK - 1:CHUNK], (CHUNK, 1)),
                                     jnp.broadcast_to(gcol[pair - 1:pair], (CHUNK, 1))], axis=0)
            kdec = kst * jnp.exp(glast - gcol)
            val_ref[2 * j, rows, :] = value[:CHUNK]
            val_ref[2 * j + 1, rows, :] = value[CHUNK:]
            kq_ref[2 * j, c] = jnp.concatenate([kcd[:CHUNK], qdec[:CHUNK]], axis=0).astype(BF16)
            kq_ref[2 * j + 1, c] = jnp.concatenate([kcd[CHUNK:], qdec[CHUNK:]], axis=0).astype(BF16)
            aq_ref[j, c] = (qk * decay).astype(BF16)
            kdt_ref[2 * j, c] = jnp.where(top, kdec, 0.0).T.astype(BF16)
            kdt_ref[2 * j + 1, c] = jnp.where(top, 0.0, kdec).T.astype(BF16)
            egl = jnp.exp(glast)
            eg_ref[pl.ds(c * nv + 2 * j, 1), :] = jnp.broadcast_to(egl[0:1], (1, LANES))
            eg_ref[pl.ds(c * nv + 2 * j + 1, 1), :] = jnp.broadcast_to(egl[CHUNK:CHUNK + 1], (1, LANES))
        return carry

    lax.fori_loop(0, nc, phase_a, 0)

    normw = normw_ref[...]

    def phase_b(c, carry):
        rows = pl.ds(pl.multiple_of(c * CHUNK, CHUNK), CHUNK)
        for j in range(nqk):
            vnew, qs_out, states = [], [], []
            for e in range(2):
                hd = 2 * j + e
                s = s_ref[hd]
                r = _dot(kq_ref[hd, c], s.astype(BF16))
                vnew.append(val_ref[hd, rows, :] - r[:CHUNK])
                qs_out.append(r[CHUNK:])
                states.append(s)
            vst = jnp.concatenate(vnew, axis=0).astype(BF16)
            o = jnp.concatenate(qs_out, axis=0) + _dot(aq_ref[j, c], vst)
            for e in range(2):
                hd = 2 * j + e
                s_ref[hd] = states[e] * eg_ref[pl.ds(c * nv + hd, 1), :] + _dot(kdt_ref[hd, c], vst)
                oe = o[e * CHUNK:(e + 1) * CHUNK]
                oe = oe * lax.rsqrt(jnp.mean(oe * oe, axis=-1, keepdims=True) + EPS) * normw
                ob_ref[rows, hd * HEAD_DIM:(hd + 1) * HEAD_DIM] = (oe * zs_ref[hd, rows, :]).astype(BF16)
        return carry

    lax.fori_loop(0, nc, phase_b, 0)

    out = _dot(ob_ref[...], wout_ref[...])
    xn = x + m[2:3] * out
    y_ref[0] = xn * lax.rsqrt(jnp.mean(xn * xn, axis=-1, keepdims=True) + EPS) * fgain_ref[...]

    @pl.when(l == pl.num_programs(1) - 1)
    def _():
        cbo_ref[0] = cc_ref[...]
        so_ref[0] = s_ref[...]


def _gdn_layer(x, mod, cbuf, s0, gain, wqkv, wz, wab, convw, alog, dtb, normw, wout, fgain, *, tile):
    nb, seqlen, d = x.shape
    nv = s0.shape[1]
    nqk = nv // 2
    cdim = wqkv.shape[1]
    vdim = wz.shape[1]
    tile = tile if seqlen % tile == 0 else seqlen
    assert tile % CHUNK == 0
    nl = seqlen // tile
    nc = tile // CHUNK
    kern = functools.partial(_gdn_kernel, tile=tile)
    scratch = [
        pltpu.VMEM((CONV_CARRY, cdim), F32),
        pltpu.VMEM((nv, HEAD_DIM, HEAD_DIM), F32),
        pltpu.VMEM((nqk, tile, HEAD_DIM), F32),
        pltpu.VMEM((nqk, tile, HEAD_DIM), F32),
        pltpu.VMEM((nv, tile, HEAD_DIM), F32),
        pltpu.VMEM((nv, tile, HEAD_DIM), F32),
        pltpu.VMEM((tile, LANES), F32),
        pltpu.VMEM((tile, LANES), F32),
        pltpu.VMEM((nc * nqk, 2 * CHUNK), F32),
        pltpu.VMEM((nv, tile, HEAD_DIM), F32),
        pltpu.VMEM((nv, nc, 2 * CHUNK, HEAD_DIM), BF16),
        pltpu.VMEM((nqk, nc, 2 * CHUNK, 2 * CHUNK), BF16),
        pltpu.VMEM((nv, nc, HEAD_DIM, 2 * CHUNK), BF16),
        pltpu.VMEM((nc * nv, LANES), F32),
        pltpu.VMEM((tile, vdim), BF16),
    ]
    return pl.pallas_call(
        kern,
        grid=(nb, nl),
        in_specs=[
            pl.BlockSpec((1, tile, d), lambda b, l: (b, l, 0)),
            pl.BlockSpec((1, 3, d), lambda b, l: (b, 0, 0)),
            pl.BlockSpec((1, CONV_CARRY, cdim), lambda b, l: (b, 0, 0)),
            pl.BlockSpec((1, nv, HEAD_DIM, HEAD_DIM), lambda b, l: (b, 0, 0, 0)),
            _const_spec((1, d)),
            _const_spec((d, cdim)),
            _const_spec((d, vdim)),
            _const_spec((d, LANES)),
            _const_spec((CONV_WIDTH, cdim)),
            _const_spec((1, LANES)),
            _const_spec((1, LANES)),
            _const_spec((1, HEAD_DIM)),
            _const_spec((vdim, d)),
            _const_spec((1, d)),
        ],
        out_specs=[
            pl.BlockSpec((1, tile, d), lambda b, l: (b, l, 0)),
            pl.BlockSpec((1, CONV_CARRY, cdim), lambda b, l: (b, 0, 0)),
            pl.BlockSpec((1, nv, HEAD_DIM, HEAD_DIM), lambda b, l: (b, 0, 0, 0)),
        ],
        out_shape=[
            jax.ShapeDtypeStruct((nb, seqlen, d), F32),
            jax.ShapeDtypeStruct((nb, CONV_CARRY, cdim), F32),
            jax.ShapeDtypeStruct((nb, nv, HEAD_DIM, HEAD_DIM), F32),
        ],
        scratch_shapes=scratch,
        compiler_params=pltpu.CompilerParams(
            dimension_semantics=("parallel", "arbitrary"), vmem_limit_bytes=VMEM_LIMIT_BYTES),
        name="gdn_layer",
    )(x, mod, cbuf, s0, gain, wqkv, wz, wab, convw, alog, dtb, normw, wout, fgain)


def _pair_order(nv):
    return jnp.concatenate([jnp.arange(0, nv, 2), jnp.arange(1, nv, 2)])


def _pad_lanes(row):
    return jnp.pad(row, (0, LANES - row.shape[0])).reshape(1, LANES)


def _trunk(x, mod, pool_buf, conv_buf, rec, start, weights, *, pool_tile, gdn_tile):
    (norm_gain, pool_w_in, pool_w_group, pool_scale, pool_w_out,
     wqkv, wz, wab, gdn_conv_w, alog, dtb, gdn_norm_w, gdn_w_out, final_gain) = weights
    d = x.shape[-1]
    buf = jnp.pad(pool_buf, ((0, 0), (POOL_CARRY - pool_buf.shape[1], 0), (0, 0)))
    x1, new_pool = _pool_layer(x, mod[0], buf, norm_gain[0].reshape(1, d), pool_w_in, pool_w_group,
                               pool_scale.reshape(1, -1), pool_w_out, start=start, tile=pool_tile)
    cbuf = jnp.pad(conv_buf, ((0, 0), (CONV_CARRY - conv_buf.shape[1], 0), (0, 0)))
    y, new_conv, new_rec = _gdn_layer(x1, mod[1], cbuf, rec, norm_gain[1].reshape(1, d), wqkv, wz, wab,
                                      gdn_conv_w, alog, dtb, gdn_norm_w.reshape(1, -1), gdn_w_out,
                                      final_gain.reshape(1, d), tile=gdn_tile)
    npool = pool_buf.shape[1]
    nconv = conv_buf.shape[1]
    return (y, new_pool[None, :, POOL_CARRY - npool:], new_conv[None, :, CONV_CARRY - nconv:], new_rec[None])


def kernel(x_prompt, x_sample, c_prompt, c_sample, state_pool, state_conv, state_rec, norm_gain, ada_w, ada_b,
           pool_w_in, pool_w_group, pool_scale, pool_w_out, gdn_w_in, gdn_conv_w, gdn_a_log, gdn_dt_bias,
           gdn_norm_w, gdn_w_out, final_gain):
    assert ada_w.shape[0] == 2 and pool_w_in.shape[0] == 1 and gdn_w_in.shape[0] == 1
    nbp = x_prompt.shape[0]
    nv = state_rec.shape[2]
    vdim = nv * HEAD_DIM
    cdim = state_conv.shape[-1]
    width = state_pool.shape[-1]

    mod = _adaln_mod(jnp.concatenate([c_prompt, c_sample], axis=0), ada_w, ada_b)
    mod = mod.transpose(0, 2, 1, 3)
    mod_p, mod_s = mod[:, :nbp], mod[:, nbp:]

    order = _pair_order(nv)
    w_in = gdn_w_in[0]
    wab = jnp.concatenate([w_in[:, cdim + vdim:cdim + vdim + nv][:, order],
                           w_in[:, cdim + vdim + nv:cdim + vdim + 2 * nv][:, order]], axis=1)
    wab = jnp.pad(wab, ((0, 0), (0, LANES - 2 * nv)))
    weights = (
        norm_gain,
        pool_w_in[0].astype(BF16), pool_w_group[0].astype(BF16), pool_scale[0], pool_w_out[0].astype(BF16),
        w_in[:, :cdim].astype(BF16), w_in[:, cdim:cdim + vdim].astype(BF16), wab,
        gdn_conv_w[0], _pad_lanes(gdn_a_log[0][order]), _pad_lanes(gdn_dt_bias[0][order]),
        gdn_norm_w[0], gdn_w_out[0].astype(BF16), final_gain,
    )

    zero_pool = jnp.zeros((nbp, POOL_CARRY - 1, width), F32)
    zero_conv = jnp.zeros((nbp, CONV_WIDTH - 1, cdim), F32)
    zero_rec = jnp.zeros((nbp, nv, HEAD_DIM, HEAD_DIM), F32)
    y_p, pool_p, conv_p, rec_p = _trunk(x_prompt, mod_p, zero_pool, zero_conv, zero_rec, 0, weights,
                                        pool_tile=512, gdn_tile=256)
    y_s, pool_s, conv_s, rec_s = _trunk(x_sample, mod_s, state_pool[0], state_conv[0], state_rec[0], PAST_LEN,
                                        weights, pool_tile=512, gdn_tile=256)
    return (y_p, y_s, pool_p, conv_p, rec_p, pool_s, conv_s, rec_s)
```

```python
import functools

import jax
import jax.numpy as jnp
from jax import lax
from jax.experimental import pallas as pl
from jax.experimental.pallas import tpu as pltpu

F32 = jnp.float32
BF16 = jnp.bfloat16
EPS = 1e-6

POOL_WINDOWS = (2, 4, 8, 16)
POOL_CARRY = 16
CONV_WIDTH = 4
CONV_CARRY = 8
HEAD_DIM = 128
CHUNK = 64
LANES = 128
VMEM_LIMIT_BYTES = 58 * 1024 * 1024
PAST_LEN = 4096


def _dot(a, b):
    return jnp.dot(a, b, preferred_element_type=F32)


def _dot_nt(a, b):
    return lax.dot_general(a, b, (((1,), (1,)), ((), ())), preferred_element_type=F32)


def _split2(a):
    hi = a.astype(BF16)
    lo = (a - hi.astype(F32)).astype(BF16)
    return hi, lo


def _split3(a):
    p1 = a.astype(BF16)
    r1 = a - p1.astype(F32)
    p2 = r1.astype(BF16)
    p3 = (r1 - p2.astype(F32)).astype(BF16)
    return p1, p2, p3


def _mm3(a, b):
    ah, al = _split2(a)
    bh, bl = _split2(b)
    return _dot(ah, bh) + (_dot(ah, bl) + _dot(al, bh))


def _mm1(a, b):
    return _dot(a.astype(BF16), b.astype(BF16))


def _half_mask(r, c, s):
    return ((r // (2 * s)) == (c // (2 * s))) & ((r % (2 * s)) >= s) & ((c % (2 * s)) < s)


def _silu(x):
    return x / (1.0 + jnp.exp(-x))


def _softplus(x):
    return jnp.maximum(x, 0.0) + jnp.log(1.0 + jnp.exp(-jnp.abs(x)))


def _const_spec(shape):
    nd = len(shape)
    return pl.BlockSpec(shape, lambda *_: (0,) * nd, pipeline_mode=pl.Buffered(1))


def _mod_kernel(c_ref, w_ref, b_ref, o_ref):
    a = _silu(c_ref[...])
    o_ref[0, 0] = _mm3(a, w_ref[0]) + b_ref[0]


def _adaln_mod(c_all, ada_w, ada_b):
    depth, d, d3 = ada_w.shape
    nb = c_all.shape[0]
    nt = d3 // d
    return pl.pallas_call(
        _mod_kernel,
        grid=(depth, nt),
        in_specs=[
            pl.BlockSpec((nb, d), lambda i, n: (0, 0)),
            pl.BlockSpec((1, d, d), lambda i, n: (i, 0, n)),
            pl.BlockSpec((1, 1, d), lambda i, n: (i, 0, n)),
        ],
        out_specs=pl.BlockSpec((1, 1, nb, d), lambda i, n: (i, n, 0, 0)),
        out_shape=jax.ShapeDtypeStruct((depth, nt, nb, d), F32),
        name="adaln_mod",
    )(c_all, ada_w, ada_b.reshape(depth, 1, d3))


def _modulated_norm(x, m, gain):
    r = lax.rsqrt(jnp.mean(x * x, axis=-1, keepdims=True) + EPS)
    return x * r * gain * (1.0 + m[1:2]) + m[0:1]


def _pool_kernel(x_ref, mod_ref, buf_ref, gain_ref, win_ref, wg_ref, sc_ref, wout_ref,
                 xo_ref, bufo_ref, carry_ref, *, tile, start):
    l = pl.program_id(1)
    width = sc_ref.shape[1]
    gdim = width // len(POOL_WINDOWS)

    @pl.when(l == 0)
    def _():
        carry_ref[...] = buf_ref[0]

    x = x_ref[0]
    m = mod_ref[0]
    h = _modulated_norm(x, m, gain_ref[...]).astype(BF16)
    pos = (start + l * tile + lax.broadcasted_iota(jnp.int32, (tile, 1), 0)).astype(F32)

    acc = jnp.zeros(x.shape, F32)
    for gi, w in enumerate(POOL_WINDOWS):
        lo = gi * gdim
        u = _dot(h, win_ref[:, lo:lo + gdim])
        z = _dot(h, win_ref[:, width + lo:width + lo + gdim])
        ext = jnp.concatenate([carry_ref[:, lo:lo + gdim], u], axis=0)
        s = ext
        for k in range(gi + 1):
            s = s + pltpu.roll(s, 1 << k, 0)
        inv_cnt = 1.0 / jnp.minimum(pos + 1.0, float(w))
        d = s[POOL_CARRY:] * inv_cnt - u
        y = _dot(d.astype(BF16), wg_ref[gi])
        y = y * sc_ref[:, lo:lo + gdim] * _silu(z)
        acc = acc + _dot(y.astype(BF16), wout_ref[lo:lo + gdim, :])
        carry_ref[:, lo:lo + gdim] = ext[tile:tile + POOL_CARRY]

    xo_ref[0] = x + m[2:3] * acc

    @pl.when(l == pl.num_programs(1) - 1)
    def _():
        bufo_ref[0] = carry_ref[...]


def _pool_layer(x, mod, buf, gain, w_in, w_group, scale, w_out, *, start, tile):
    nb, seqlen, d = x.shape
    width = scale.shape[-1]
    ng, gdim, _ = w_group.shape
    tile = tile if seqlen % tile == 0 else seqlen
    nl = seqlen // tile
    kern = functools.partial(_pool_kernel, tile=tile, start=start)
    return pl.pallas_call(
        kern,
        grid=(nb, nl),
        in_specs=[
            pl.BlockSpec((1, tile, d), lambda b, l: (b, l, 0)),
            pl.BlockSpec((1, 3, d), lambda b, l: (b, 0, 0)),
            pl.BlockSpec((1, POOL_CARRY, width), lambda b, l: (b, 0, 0)),
            _const_spec((1, d)),
            _const_spec((d, 2 * width)),
            _const_spec((ng, gdim, gdim)),
            _const_spec((1, width)),
            _const_spec((width, d)),
        ],
        out_specs=[
            pl.BlockSpec((1, tile, d), lambda b, l: (b, l, 0)),
            pl.BlockSpec((1, POOL_CARRY, width), lambda b, l: (b, 0, 0)),
        ],
        out_shape=[
            jax.ShapeDtypeStruct((nb, seqlen, d), F32),
            jax.ShapeDtypeStruct((nb, POOL_CARRY, width), F32),
        ],
        scratch_shapes=[pltpu.VMEM((POOL_CARRY, width), F32)],
        compiler_params=pltpu.CompilerParams(
            dimension_semantics=("parallel", "arbitrary"), vmem_limit_bytes=VMEM_LIMIT_BYTES),
        name="pool_layer",
    )(x, mod, buf, gain, w_in, w_group, scale, w_out)


def _gdn_kernel(x_ref, mod_ref, cbuf_ref, s0_ref, gain_ref, wqkv_ref, wz_ref, wab_ref, convw_ref,
                alog_ref, dtb_ref, normw_ref, wout_ref, fgain_ref,
                y_ref, cbo_ref, so_ref,
                cc_ref, s_ref, qs_ref, ks_ref, vs_ref, zs_ref, gc_ref, bb_ref, gr_ref,
                val_ref, kq_ref, aq_ref, kdt_ref, eg_ref, ob_ref, ext_ref, *, tile):
    l = pl.program_id(1)
    nqk = qs_ref.shape[0]
    nv = vs_ref.shape[0]
    nc = tile // CHUNK
    pair = 2 * CHUNK

    @pl.when(l == 0)
    def _():
        cc_ref[...] = cbuf_ref[0]
        s_ref[...] = s0_ref[0]

    x = x_ref[0]
    m = mod_ref[0]
    hf = _modulated_norm(x, m, gain_ref[...])
    h, h_lo = _split2(hf)

    nblk = wqkv_ref.shape[1] // (2 * HEAD_DIM)
    for blk in range(nblk):
        cs = slice(blk * 2 * HEAD_DIM, (blk + 1) * 2 * HEAD_DIM)
        pre = _dot(h, wqkv_ref[:, cs])
        ext_ref[0:CONV_CARRY, :] = cc_ref[:, cs]
        ext_ref[CONV_CARRY:, :] = pre
        cw = convw_ref[:, cs]
        acc = pre * cw[CONV_WIDTH - 1:CONV_WIDTH]
        for t in range(CONV_WIDTH - 1):
            acc = acc + ext_ref[pl.ds(CONV_CARRY - (CONV_WIDTH - 1) + t, tile), :] * cw[t:t + 1]
        conv = _silu(acc)
        cc_ref[:, cs] = ext_ref[pl.ds(tile, CONV_CARRY), :]
        for e in range(2):
            head = 2 * blk + e
            part = conv[:, e * HEAD_DIM:(e + 1) * HEAD_DIM]
            if head < 2 * nqk:
                part = part * lax.rsqrt(jnp.sum(part * part, axis=-1, keepdims=True) + EPS)
                if head < nqk:
                    qs_ref[head] = part * (HEAD_DIM ** -0.5)
                else:
                    ks_ref[head - nqk] = part
            else:
                vs_ref[head - 2 * nqk] = part

    for blk in range(nv // 2):
        zz = _silu(_dot(h, wz_ref[:, blk * 2 * HEAD_DIM:(blk + 1) * 2 * HEAD_DIM]))
        zs_ref[2 * blk] = zz[:, :HEAD_DIM]
        zs_ref[2 * blk + 1] = zz[:, HEAD_DIM:]

    wab_hi, wab_lo = _split2(wab_ref[...])
    ab = _dot(h, wab_hi) + (_dot(h, wab_lo) + _dot(h_lo, wab_hi))
    bb_ref[...] = 1.0 / (1.0 + jnp.exp(-ab))
    g = -jnp.exp(alog_ref[...]) * _softplus(ab + dtb_ref[...])
    ri = lax.broadcasted_iota(jnp.int32, (tile, tile), 0)
    ci = lax.broadcasted_iota(jnp.int32, (tile, tile), 1)
    tril = ((ri >= ci) & ((ri // CHUNK) == (ci // CHUNK))).astype(BF16)
    g1, g2, g3 = _split3(g)
    gc = _dot(tril, g1) + (_dot(tril, g2) + _dot(tril, g3))
    gc_ref[...] = gc
    er = lax.broadcasted_iota(jnp.int32, (LANES, LANES), 0)
    ec = lax.broadcasted_iota(jnp.int32, (LANES, LANES), 1)
    eye = (er == ec).astype(BF16)
    c1, c2, c3 = _split3(gc)
    gct = _dot_nt(eye, c1) + (_dot_nt(eye, c2) + _dot_nt(eye, c3))
    for c in range(nc):
        gr_ref[c * nqk:(c + 1) * nqk, 0:CHUNK] = gct[0:nqk, c * CHUNK:(c + 1) * CHUNK]
        gr_ref[c * nqk:(c + 1) * nqk, CHUNK:pair] = gct[nqk:nv, c * CHUNK:(c + 1) * CHUNK]

    pr = lax.broadcasted_iota(jnp.int32, (pair, pair), 0)
    pc = lax.broadcasted_iota(jnp.int32, (pair, pair), 1)
    same = (pr // CHUNK) == (pc // CHUNK)
    incl = same & (pr >= pc)
    strict = same & (pr > pc)
    ident = (pr == pc).astype(F32)
    top = lax.broadcasted_iota(jnp.int32, (pair, HEAD_DIM), 0) < CHUNK

    levels = []
    s = 2
    while s < CHUNK:
        levels.append(_half_mask(pr, pc, s))
        s *= 2
    base_mask = _half_mask(pr, pc, 1)
    pairs = range(nqk)

    def phase_a(c, carry):
        rows = pl.ds(c * CHUNK, CHUNK)
        kst, qst, vst, gcol, bcol, decay, lmat, qk, xinv = [], [], [], [], [], [], [], [], []
        for j in pairs:
            k = ks_ref[j, rows, :]
            q = qs_ref[j, rows, :]
            kst.append(jnp.concatenate([k, k], axis=0))
            qst.append(jnp.concatenate([q, q], axis=0))
            vst.append(jnp.concatenate([vs_ref[2 * j, rows, :], vs_ref[2 * j + 1, rows, :]], axis=0))
            gcol.append(jnp.concatenate([gc_ref[rows, j:j + 1], gc_ref[rows, nqk + j:nqk + j + 1]], axis=0))
            bcol.append(jnp.concatenate([bb_ref[rows, nv + j:nv + j + 1],
                                         bb_ref[rows, nv + nqk + j:nv + nqk + j + 1]], axis=0))
            grow = gr_ref[pl.ds(c * nqk + j, 1), :]
            decay.append(jnp.where(incl, jnp.exp(jnp.where(incl, gcol[j] - grow, 0.0)), 0.0))
        for j in pairs:
            kb = kst[j].astype(BF16)
            kk = _dot_nt(kb, kb)
            qk.append(_dot_nt(qst[j].astype(BF16), kb))
            lmat.append(jnp.where(strict, bcol[j] * kk * decay[j], 0.0))
            xinv.append(ident - jnp.where(base_mask, lmat[j], 0.0))
        for mask in levels:
            t = [_mm1(xinv[j], jnp.where(mask, lmat[j], 0.0)) for j in pairs]
            xinv = [xinv[j] - _mm1(t[j], xinv[j]) for j in pairs]
        sol, eg = [], []
        for j in pairs:
            eg.append(jnp.exp(gcol[j]))
            rhs = jnp.concatenate([vst[j] * bcol[j], kst[j] * (bcol[j] * eg[j])], axis=1)
            sol.append(_mm1(xinv[j], rhs))
        for j in pairs:
            value = sol[j][:, :HEAD_DIM]
            kcd = sol[j][:, HEAD_DIM:]
            qdec = qst[j] * eg[j]
            glast = jnp.concatenate([jnp.broadcast_to(gcol[j][CHUNK - 1:CHUNK], (CHUNK, 1)),
                                     jnp.broadcast_to(gcol[j][pair - 1:pair], (CHUNK, 1))], axis=0)
            kdec = kst[j] * jnp.exp(glast - gcol[j])
            val_ref[2 * j, rows, :] = value[:CHUNK]
            val_ref[2 * j + 1, rows, :] = value[CHUNK:]
            kq_ref[2 * j, c] = jnp.concatenate([kcd[:CHUNK], qdec[:CHUNK]], axis=0).astype(BF16)
            kq_ref[2 * j + 1, c] = jnp.concatenate([kcd[CHUNK:], qdec[CHUNK:]], axis=0).astype(BF16)
            aq_ref[j, c] = (qk[j] * decay[j]).astype(BF16)
            kdt_ref[2 * j, c] = jnp.where(top, kdec, 0.0).T.astype(BF16)
            kdt_ref[2 * j + 1, c] = jnp.where(top, 0.0, kdec).T.astype(BF16)
            egl = jnp.exp(glast)
            eg_ref[pl.ds(c * nv + 2 * j, 1), :] = jnp.broadcast_to(egl[0:1], (1, LANES))
            eg_ref[pl.ds(c * nv + 2 * j + 1, 1), :] = jnp.broadcast_to(egl[CHUNK:CHUNK + 1], (1, LANES))
        return carry

    for c in range(nc):
        phase_a(c, 0)

    normw = normw_ref[...]
    heads = range(nv)

    def phase_b(c, carry):
        rows = pl.ds(c * CHUNK, CHUNK)
        states = [s_ref[hd] for hd in heads]
        r = [_dot(kq_ref[hd, c], states[hd].astype(BF16)) for hd in heads]
        vnew = [val_ref[hd, rows, :] - r[hd][:CHUNK] for hd in heads]
        vst = [jnp.concatenate([vnew[2 * j], vnew[2 * j + 1]], axis=0).astype(BF16) for j in pairs]
        o = [jnp.concatenate([r[2 * j][CHUNK:], r[2 * j + 1][CHUNK:]], axis=0) + _dot(aq_ref[j, c], vst[j])
             for j in pairs]
        upd = [_dot(kdt_ref[hd, c], vst[hd // 2]) for hd in heads]
        for hd in heads:
            s_ref[hd] = states[hd] * eg_ref[pl.ds(c * nv + hd, 1), :] + upd[hd]
            oe = o[hd // 2][(hd % 2) * CHUNK:(hd % 2 + 1) * CHUNK]
            oe = oe * lax.rsqrt(jnp.mean(oe * oe, axis=-1, keepdims=True) + EPS) * normw
            ob_ref[rows, hd * HEAD_DIM:(hd + 1) * HEAD_DIM] = (oe * zs_ref[hd, rows, :]).astype(BF16)
        return carry

    for c in range(nc):
        phase_b(c, 0)

    out = _dot(ob_ref[...], wout_ref[...])
    xn = x + m[2:3] * out
    y_ref[0] = xn * lax.rsqrt(jnp.mean(xn * xn, axis=-1, keepdims=True) + EPS) * fgain_ref[...]

    @pl.when(l == pl.num_programs(1) - 1)
    def _():
        cbo_ref[0] = cc_ref[...]
        so_ref[0] = s_ref[...]


def _gdn_layer(x, mod, cbuf, s0, gain, wqkv, wz, wab, convw, alog, dtb, normw, wout, fgain, *, tile):
    nb, seqlen, d = x.shape
    nv = s0.shape[1]
    nqk = nv // 2
    cdim = wqkv.shape[1]
    vdim = wz.shape[1]
    tile = tile if seqlen % tile == 0 else seqlen
    assert tile % CHUNK == 0
    nl = seqlen // tile
    nc = tile // CHUNK
    kern = functools.partial(_gdn_kernel, tile=tile)
    scratch = [
        pltpu.VMEM((CONV_CARRY, cdim), F32),
        pltpu.VMEM((nv, HEAD_DIM, HEAD_DIM), F32),
        pltpu.VMEM((nqk, tile, HEAD_DIM), F32),
        pltpu.VMEM((nqk, tile, HEAD_DIM), F32),
        pltpu.VMEM((nv, tile, HEAD_DIM), F32),
        pltpu.VMEM((nv, tile, HEAD_DIM), F32),
        pltpu.VMEM((tile, LANES), F32),
        pltpu.VMEM((tile, LANES), F32),
        pltpu.VMEM((nc * nqk, 2 * CHUNK), F32),
        pltpu.VMEM((nv, tile, HEAD_DIM), F32),
        pltpu.VMEM((nv, nc, 2 * CHUNK, HEAD_DIM), BF16),
        pltpu.VMEM((nqk, nc, 2 * CHUNK, 2 * CHUNK), BF16),
        pltpu.VMEM((nv, nc, HEAD_DIM, 2 * CHUNK), BF16),
        pltpu.VMEM((nc * nv, LANES), F32),
        pltpu.VMEM((tile, vdim), BF16),
        pltpu.VMEM((CONV_CARRY + tile, 2 * HEAD_DIM), F32),
    ]
    return pl.pallas_call(
        kern,
        grid=(nb, nl),
        in_specs=[
            pl.BlockSpec((1, tile, d), lambda b, l: (b, l, 0)),
            pl.BlockSpec((1, 3, d), lambda b, l: (b, 0, 0)),
            pl.BlockSpec((1, CONV_CARRY, cdim), lambda b, l: (b, 0, 0)),
            pl.BlockSpec((1, nv, HEAD_DIM, HEAD_DIM), lambda b, l: (b, 0, 0, 0)),
            _const_spec((1, d)),
            _const_spec((d, cdim)),
            _const_spec((d, vdim)),
            _const_spec((d, LANES)),
            _const_spec((CONV_WIDTH, cdim)),
            _const_spec((1, LANES)),
            _const_spec((1, LANES)),
            _const_spec((1, HEAD_DIM)),
            _const_spec((vdim, d)),
            _const_spec((1, d)),
        ],
        out_specs=[
            pl.BlockSpec((1, tile, d), lambda b, l: (b, l, 0)),
            pl.BlockSpec((1, CONV_CARRY, cdim), lambda b, l: (b, 0, 0)),
            pl.BlockSpec((1, nv, HEAD_DIM, HEAD_DIM), lambda b, l: (b, 0, 0, 0)),
        ],
        out_shape=[
            jax.ShapeDtypeStruct((nb, seqlen, d), F32),
            jax.ShapeDtypeStruct((nb, CONV_CARRY, cdim), F32),
            jax.ShapeDtypeStruct((nb, nv, HEAD_DIM, HEAD_DIM), F32),
        ],
        scratch_shapes=scratch,
        compiler_params=pltpu.CompilerParams(
            dimension_semantics=("parallel", "arbitrary"), vmem_limit_bytes=VMEM_LIMIT_BYTES),
        name="gdn_layer",
    )(x, mod, cbuf, s0, gain, wqkv, wz, wab, convw, alog, dtb, normw, wout, fgain)


def _pair_order(nv):
    return jnp.concatenate([jnp.arange(0, nv, 2), jnp.arange(1, nv, 2)])


def _pad_lanes(row):
    return jnp.pad(row, (0, LANES - row.shape[0])).reshape(1, LANES)


def _trunk(x, mod, pool_buf, conv_buf, rec, start, weights, *, pool_tile, gdn_tile):
    (norm_gain, pool_w_in, pool_w_group, pool_scale, pool_w_out,
     wqkv, wz, wab, gdn_conv_w, alog, dtb, gdn_norm_w, gdn_w_out, final_gain) = weights
    d = x.shape[-1]
    buf = jnp.pad(pool_buf, ((0, 0), (POOL_CARRY - pool_buf.shape[1], 0), (0, 0)))
    x1, new_pool = _pool_layer(x, mod[0], buf, norm_gain[0].reshape(1, d), pool_w_in, pool_w_group,
                               pool_scale.reshape(1, -1), pool_w_out, start=start, tile=pool_tile)
    cbuf = jnp.pad(conv_buf, ((0, 0), (CONV_CARRY - conv_buf.shape[1], 0), (0, 0)))
    y, new_conv, new_rec = _gdn_layer(x1, mod[1], cbuf, rec, norm_gain[1].reshape(1, d), wqkv, wz, wab,
                                      gdn_conv_w, alog, dtb, gdn_norm_w.reshape(1, -1), gdn_w_out,
                                      final_gain.reshape(1, d), tile=gdn_tile)
    npool = pool_buf.shape[1]
    nconv = conv_buf.shape[1]
    return (y, new_pool[None, :, POOL_CARRY - npool:], new_conv[None, :, CONV_CARRY - nconv:], new_rec[None])


def kernel(x_prompt, x_sample, c_prompt, c_sample, state_pool, state_conv, state_rec, norm_gain, ada_w, ada_b,
           pool_w_in, pool_w_group, pool_scale, pool_w_out, gdn_w_in, gdn_conv_w, gdn_a_log, gdn_dt_bias,
           gdn_norm_w, gdn_w_out, final_gain):
    assert ada_w.shape[0] == 2 and pool_w_in.shape[0] == 1 and gdn_w_in.shape[0] == 1
    nbp = x_prompt.shape[0]
    nv = state_rec.shape[2]
    vdim = nv * HEAD_DIM
    cdim = state_conv.shape[-1]
    width = state_pool.shape[-1]

    mod = _adaln_mod(jnp.concatenate([c_prompt, c_sample], axis=0), ada_w, ada_b)
    mod = mod.transpose(0, 2, 1, 3)
    mod_p, mod_s = mod[:, :nbp], mod[:, nbp:]

    order = _pair_order(nv)
    w_in = gdn_w_in[0]
    wab = jnp.concatenate([w_in[:, cdim + vdim:cdim + vdim + nv][:, order],
                           w_in[:, cdim + vdim + nv:cdim + vdim + 2 * nv][:, order]], axis=1)
    wab = jnp.pad(wab, ((0, 0), (0, LANES - 2 * nv)))
    weights = (
        norm_gain,
        pool_w_in[0].astype(BF16), pool_w_group[0].astype(BF16), pool_scale[0], pool_w_out[0].astype(BF16),
        w_in[:, :cdim].astype(BF16), w_in[:, cdim:cdim + vdim].astype(BF16), wab,
        gdn_conv_w[0], _pad_lanes(gdn_a_log[0][order]), _pad_lanes(gdn_dt_bias[0][order]),
        gdn_norm_w[0], gdn_w_out[0].astype(BF16), final_gain,
    )

    zero_pool = jnp.zeros((nbp, POOL_CARRY - 1, width), F32)
    zero_conv = jnp.zeros((nbp, CONV_WIDTH - 1, cdim), F32)
    zero_rec = jnp.zeros((nbp, nv, HEAD_DIM, HEAD_DIM), F32)
    y_p, pool_p, conv_p, rec_p = _trunk(x_prompt, mod_p, zero_pool, zero_conv, zero_rec, 0, weights,
                                        pool_tile=512, gdn_tile=256)
    y_s, pool_s, conv_s, rec_s = _trunk(x_sample, mod_s, state_pool[0], state_conv[0], state_rec[0], PAST_LEN,
                                        weights, pool_tile=512, gdn_tile=256)
    return (y_p, y_s, pool_p, conv_p, rec_p, pool_s, conv_s, rec_s)
```

```python
import functools

import jax
import jax.numpy as jnp
from jax import lax
from jax.experimental import pallas as pl
from jax.experimental.pallas import tpu as pltpu

F32 = jnp.float32
BF16 = jnp.bfloat16
EPS = 1e-6

POOL_WINDOWS = (2, 4, 8, 16)
POOL_CARRY = 16
CONV_WIDTH = 4
CONV_CARRY = 8
HEAD_DIM = 128
CHUNK = 64
GROUP_CHUNKS = 2
LANES = 128
VMEM_LIMIT_BYTES = 58 * 1024 * 1024
PAST_LEN = 4096


def _dot(a, b):
    return jnp.dot(a, b, preferred_element_type=F32)


def _dot_nt(a, b):
    return lax.dot_general(a, b, (((1,), (1,)), ((), ())), preferred_element_type=F32)


def _split2(a):
    hi = a.astype(BF16)
    lo = (a - hi.astype(F32)).astype(BF16)
    return hi, lo


def _split3(a):
    p1 = a.astype(BF16)
    r1 = a - p1.astype(F32)
    p2 = r1.astype(BF16)
    p3 = (r1 - p2.astype(F32)).astype(BF16)
    return p1, p2, p3


def _mm3(a, b):
    ah, al = _split2(a)
    bh, bl = _split2(b)
    return _dot(ah, bh) + (_dot(ah, bl) + _dot(al, bh))


def _mm1(a, b):
    return _dot(a.astype(BF16), b.astype(BF16))


def _half_mask(r, c, s):
    return ((r // (2 * s)) == (c // (2 * s))) & ((r % (2 * s)) >= s) & ((c % (2 * s)) < s)


def _silu(x):
    return x / (1.0 + jnp.exp(-x))


def _softplus(x):
    return jnp.maximum(x, 0.0) + jnp.log(1.0 + jnp.exp(-jnp.abs(x)))


def _const_spec(shape):
    nd = len(shape)
    return pl.BlockSpec(shape, lambda *_: (0,) * nd, pipeline_mode=pl.Buffered(1))


def _mod_kernel(c_ref, w_ref, b_ref, o_ref):
    a = _silu(c_ref[...])
    o_ref[0, 0] = _mm3(a, w_ref[0]) + b_ref[0]


def _adaln_mod(c_all, ada_w, ada_b):
    depth, d, d3 = ada_w.shape
    nb = c_all.shape[0]
    nt = d3 // d
    return pl.pallas_call(
        _mod_kernel,
        grid=(depth, nt),
        in_specs=[
            pl.BlockSpec((nb, d), lambda i, n: (0, 0)),
            pl.BlockSpec((1, d, d), lambda i, n: (i, 0, n)),
            pl.BlockSpec((1, 1, d), lambda i, n: (i, 0, n)),
        ],
        out_specs=pl.BlockSpec((1, 1, nb, d), lambda i, n: (i, n, 0, 0)),
        out_shape=jax.ShapeDtypeStruct((depth, nt, nb, d), F32),
        name="adaln_mod",
    )(c_all, ada_w, ada_b.reshape(depth, 1, d3))


def _modulated_norm(x, m, gain):
    r = lax.rsqrt(jnp.mean(x * x, axis=-1, keepdims=True) + EPS)
    return x * r * gain * (1.0 + m[1:2]) + m[0:1]


def _pool_kernel(x_ref, mod_ref, buf_ref, gain_ref, win_ref, wg_ref, sc_ref, wout_ref,
                 xo_ref, bufo_ref, carry_ref, *, tile, start):
    l = pl.program_id(1)
    width = sc_ref.shape[1]
    gdim = width // len(POOL_WINDOWS)

    @pl.when(l == 0)
    def _():
        carry_ref[...] = buf_ref[0]

    x = x_ref[0]
    m = mod_ref[0]
    h = _modulated_norm(x, m, gain_ref[...]).astype(BF16)
    pos = (start + l * tile + lax.broadcasted_iota(jnp.int32, (tile, 1), 0)).astype(F32)

    acc = jnp.zeros(x.shape, F32)
    for gi, w in enumerate(POOL_WINDOWS):
        lo = gi * gdim
        u = _dot(h, win_ref[:, lo:lo + gdim])
        z = _dot(h, win_ref[:, width + lo:width + lo + gdim])
        ext = jnp.concatenate([carry_ref[:, lo:lo + gdim], u], axis=0)
        s = ext
        for k in range(gi + 1):
            s = s + pltpu.roll(s, 1 << k, 0)
        inv_cnt = 1.0 / jnp.minimum(pos + 1.0, float(w))
        d = s[POOL_CARRY:] * inv_cnt - u
        y = _dot(d.astype(BF16), wg_ref[gi])
        y = y * sc_ref[:, lo:lo + gdim] * _silu(z)
        acc = acc + _dot(y.astype(BF16), wout_ref[lo:lo + gdim, :])
        carry_ref[:, lo:lo + gdim] = ext[tile:tile + POOL_CARRY]

    xo_ref[0] = x + m[2:3] * acc

    @pl.when(l == pl.num_programs(1) - 1)
    def _():
        bufo_ref[0] = carry_ref[...]


def _pool_layer(x, mod, buf, gain, w_in, w_group, scale, w_out, *, start, tile):
    nb, seqlen, d = x.shape
    width = scale.shape[-1]
    ng, gdim, _ = w_group.shape
    tile = tile if seqlen % tile == 0 else seqlen
    nl = seqlen // tile
    kern = functools.partial(_pool_kernel, tile=tile, start=start)
    return pl.pallas_call(
        kern,
        grid=(nb, nl),
        in_specs=[
            pl.BlockSpec((1, tile, d), lambda b, l: (b, l, 0)),
            pl.BlockSpec((1, 3, d), lambda b, l: (b, 0, 0)),
            pl.BlockSpec((1, POOL_CARRY, width), lambda b, l: (b, 0, 0)),
            _const_spec((1, d)),
            _const_spec((d, 2 * width)),
            _const_spec((ng, gdim, gdim)),
            _const_spec((1, width)),
            _const_spec((width, d)),
        ],
        out_specs=[
            pl.BlockSpec((1, tile, d), lambda b, l: (b, l, 0)),
            pl.BlockSpec((1, POOL_CARRY, width), lambda b, l: (b, 0, 0)),
        ],
        out_shape=[
            jax.ShapeDtypeStruct((nb, seqlen, d), F32),
            jax.ShapeDtypeStruct((nb, POOL_CARRY, width), F32),
        ],
        scratch_shapes=[pltpu.VMEM((POOL_CARRY, width), F32)],
        compiler_params=pltpu.CompilerParams(
            dimension_semantics=("parallel", "arbitrary"), vmem_limit_bytes=VMEM_LIMIT_BYTES),
        name="pool_layer",
    )(x, mod, buf, gain, w_in, w_group, scale, w_out)


def _gdn_kernel(x_ref, mod_ref, cbuf_ref, s0_ref, gain_ref, wqkv_ref, wz_ref, wab_ref, convw_ref,
                alog_ref, dtb_ref, normw_ref, wout_ref, fgain_ref,
                y_ref, cbo_ref, so_ref,
                cc_ref, s_ref, qs_ref, ks_ref, vs_ref, zs_ref, gc_ref, bb_ref, gr_ref, br_ref, gl_ref,
                val_ref, kq_ref, akd_ref, eg_ref, ob_ref, ext_ref, *, tile):
    l = pl.program_id(1)
    nqk = qs_ref.shape[0]
    nv = vs_ref.shape[0]
    nc = tile // CHUNK
    pair = 2 * CHUNK

    @pl.when(l == 0)
    def _():
        cc_ref[...] = cbuf_ref[0]
        s_ref[...] = s0_ref[0]

    x = x_ref[0]
    m = mod_ref[0]
    hf = _modulated_norm(x, m, gain_ref[...])
    h, h_lo = _split2(hf)

    nblk = wqkv_ref.shape[1] // (2 * HEAD_DIM)
    for blk in range(nblk):
        cs = slice(blk * 2 * HEAD_DIM, (blk + 1) * 2 * HEAD_DIM)
        pre = _dot(h, wqkv_ref[:, cs])
        ext_ref[0:CONV_CARRY, :] = cc_ref[:, cs]
        ext_ref[CONV_CARRY:, :] = pre
        cw = convw_ref[:, cs]
        acc = pre * cw[CONV_WIDTH - 1:CONV_WIDTH]
        for t in range(CONV_WIDTH - 1):
            acc = acc + ext_ref[pl.ds(CONV_CARRY - (CONV_WIDTH - 1) + t, tile), :] * cw[t:t + 1]
        conv = _silu(acc)
        cc_ref[:, cs] = ext_ref[pl.ds(tile, CONV_CARRY), :]
        for e in range(2):
            head = 2 * blk + e
            part = conv[:, e * HEAD_DIM:(e + 1) * HEAD_DIM]
            if head < 2 * nqk:
                part = part * lax.rsqrt(jnp.sum(part * part, axis=-1, keepdims=True) + EPS)
                if head < nqk:
                    qs_ref[head] = part * (HEAD_DIM ** -0.5)
                else:
                    ks_ref[head - nqk] = part
            else:
                vs_ref[head - 2 * nqk] = part

    def z_block(blk):
        zz = _silu(_dot(h, wz_ref[:, blk * 2 * HEAD_DIM:(blk + 1) * 2 * HEAD_DIM]))
        zs_ref[2 * blk] = zz[:, :HEAD_DIM]
        zs_ref[2 * blk + 1] = zz[:, HEAD_DIM:]

    z_blocks = list(range(nv // 2))
    assert len(z_blocks) == nqk

    wab_hi, wab_lo = _split2(wab_ref[...])
    ab = _dot(h, wab_hi) + (_dot(h, wab_lo) + _dot(h_lo, wab_hi))
    beta = 1.0 / (1.0 + jnp.exp(-ab))
    bb_ref[...] = beta
    g = -jnp.exp(alog_ref[...]) * _softplus(ab + dtb_ref[...])
    ri = lax.broadcasted_iota(jnp.int32, (tile, tile), 0)
    ci = lax.broadcasted_iota(jnp.int32, (tile, tile), 1)
    tril = ((ri >= ci) & ((ri // CHUNK) == (ci // CHUNK))).astype(BF16)
    g1, g2, g3 = _split3(g)
    gc = _dot(tril, g1) + (_dot(tril, g2) + _dot(tril, g3))
    gc_ref[...] = gc
    er = lax.broadcasted_iota(jnp.int32, (LANES, LANES), 0)
    ec = lax.broadcasted_iota(jnp.int32, (LANES, LANES), 1)
    eye = (er == ec).astype(BF16)

    def transpose_rows(a):
        a1, a2, a3 = _split3(a)
        return _dot_nt(eye, a1) + (_dot_nt(eye, a2) + _dot_nt(eye, a3))

    gct = transpose_rows(gc)
    bt = transpose_rows(beta)
    for c in range(nc):
        pr = slice(c * nqk, (c + 1) * nqk)
        cs = slice(c * CHUNK, (c + 1) * CHUNK)
        gr_ref[pr, 0:CHUNK] = gct[0:nqk, cs]
        gr_ref[pr, CHUNK:pair] = gct[nqk:nv, cs]
        br_ref[pr, 0:CHUNK] = bt[nv:nv + nqk, cs]
        br_ref[pr, CHUNK:pair] = bt[nv + nqk:2 * nv, cs]
        last = gct[0:nv, (c + 1) * CHUNK - 1:(c + 1) * CHUNK]
        gl_ref[pr, 0:CHUNK] = jnp.broadcast_to(last[0:nqk], (nqk, CHUNK))
        gl_ref[pr, CHUNK:pair] = jnp.broadcast_to(last[nqk:nv], (nqk, CHUNK))
        eg_ref[c * nv:(c + 1) * nv, :] = jnp.broadcast_to(jnp.exp(last), (nv, LANES))

    ii = lax.broadcasted_iota(jnp.int32, (CHUNK, pair), 0)
    lane = lax.broadcasted_iota(jnp.int32, (CHUNK, pair), 1)
    jj = lane % CHUNK
    left = lane < CHUNK
    incl = ii >= jj
    strict = ii > jj
    ident = (ii == jj).astype(F32)
    levels = []
    s = 2
    while s < CHUNK:
        levels.append(_half_mask(ii, jj, s))
        s *= 2
    base_mask = _half_mask(ii, jj, 1)
    pairs = range(nqk)

    def blockdiag(a):
        ab = a.astype(BF16)
        zero = jnp.zeros_like(ab)
        return jnp.concatenate([jnp.where(left, ab, zero), jnp.where(left, zero, ab)], axis=0)

    left2 = lax.broadcasted_iota(jnp.int32, (pair, pair), 1) < CHUNK

    def setup(c, j):
        rows = pl.ds(c * CHUNK, CHUNK)
        k = ks_ref[j, rows, :]
        q = qs_ref[j, rows, :]
        kst = jnp.concatenate([k, k], axis=0)
        qst = jnp.concatenate([q, q], axis=0)
        vst = jnp.concatenate([vs_ref[2 * j, rows, :], vs_ref[2 * j + 1, rows, :]], axis=0)
        ga, gb = gc_ref[rows, j:j + 1], gc_ref[rows, nqk + j:nqk + j + 1]
        ba, bb = bb_ref[rows, nv + j:nv + j + 1], bb_ref[rows, nv + nqk + j:nv + nqk + j + 1]
        gcol = jnp.concatenate([ga, gb], axis=0)
        grow = gr_ref[pl.ds(c * nqk + j, 1), :]
        brow = br_ref[pl.ds(c * nqk + j, 1), :]
        dec = jnp.where(incl, jnp.exp(jnp.where(incl, jnp.where(left, ga, gb) - grow, 0.0)), 0.0)
        kb = kst.astype(BF16)
        kkqk = _dot_nt(jnp.concatenate([k, q], axis=0).astype(BF16), kb)
        kk, qk = kkqk[:CHUNK], kkqk[CHUNK:]
        lmat = jnp.where(strict, jnp.where(left, ba, bb) * kk * dec, 0.0)
        xinv = ident - jnp.where(base_mask, lmat, 0.0)
        kdt = (kst.T * jnp.exp(gl_ref[pl.ds(c * nqk + j, 1), :] - grow)).astype(BF16)
        zero = jnp.zeros_like(kdt)
        akd_ref[j, c, 0:pair, :] = blockdiag(qk * dec)
        akd_ref[j, c, pair:2 * pair, :] = jnp.where(left2, kdt, zero)
        akd_ref[j, c, 2 * pair:3 * pair, :] = jnp.where(left2, zero, kdt)
        return dict(kst=kst, qst=qst, vst=vst, gcol=gcol, grow=grow, brow=brow, lmat=lmat, xinv=xinv)

    def finish(c, j, ch, xinv):
        rows = pl.ds(c * CHUNK, CHUNK)
        xb = xinv * ch["brow"]
        value = _dot(blockdiag(xb), ch["vst"].astype(BF16))
        kcd = _dot(blockdiag(xb * jnp.exp(ch["grow"])), ch["kst"].astype(BF16))
        qdec = ch["qst"] * jnp.exp(ch["gcol"])
        val_ref[2 * j, rows, :] = value[:CHUNK]
        val_ref[2 * j + 1, rows, :] = value[CHUNK:]
        kq_ref[2 * j, c] = jnp.concatenate([kcd[:CHUNK], qdec[:CHUNK]], axis=0).astype(BF16)
        kq_ref[2 * j + 1, c] = jnp.concatenate([kcd[CHUNK:], qdec[CHUNK:]], axis=0).astype(BF16)

    groups = [[(c, j) for c in range(g, min(g + GROUP_CHUNKS, nc)) for j in pairs]
              for g in range(0, nc, GROUP_CHUNKS)]
    cur = []
    for n, (c, j) in enumerate(groups[0]):
        if n < len(z_blocks):
            z_block(z_blocks[n])
        cur.append(setup(c, j))
    for z in z_blocks[len(groups[0]):]:
        z_block(z)
    for g, chains in enumerate(groups):
        todo = list(groups[g + 1]) if g + 1 < len(groups) else []
        per_slot = -(-len(todo) // (2 * len(levels)))
        nxt = []
        xinv = [ch["xinv"] for ch in cur]
        nch = range(len(chains))
        for mask in levels:
            t = [_dot(xinv[n].astype(BF16), blockdiag(jnp.where(mask, cur[n]["lmat"], 0.0))) for n in nch]
            for _ in range(min(per_slot, len(todo))):
                nxt.append(setup(*todo.pop(0)))
            xinv = [xinv[n] - _dot(t[n].astype(BF16), blockdiag(xinv[n])) for n in nch]
            for _ in range(min(per_slot, len(todo))):
                nxt.append(setup(*todo.pop(0)))
        while todo:
            nxt.append(setup(*todo.pop(0)))
        for n, (c, j) in enumerate(chains):
            finish(c, j, cur[n], xinv[n])
        cur = nxt

    normw = normw_ref[...]
    heads = range(nv)

    def phase_b(c, carry):
        rows = pl.ds(c * CHUNK, CHUNK)
        states = [s_ref[hd] for hd in heads]
        r = [_dot(kq_ref[hd, c], states[hd].astype(BF16)) for hd in heads]
        vnew = [val_ref[hd, rows, :] - r[hd][:CHUNK] for hd in heads]
        vst = [jnp.concatenate([vnew[2 * j], vnew[2 * j + 1]], axis=0).astype(BF16) for j in pairs]
        au = [_dot(akd_ref[j, c], vst[j]) for j in pairs]
        for hd in heads:
            j, e = hd // 2, hd % 2
            upd = au[j][(1 + e) * pair:(2 + e) * pair]
            s_ref[hd] = states[hd] * eg_ref[pl.ds(c * nv + j + nqk * e, 1), :] + upd
            oe = r[hd][CHUNK:] + au[j][e * CHUNK:(e + 1) * CHUNK]
            oe = oe * lax.rsqrt(jnp.mean(oe * oe, axis=-1, keepdims=True) + EPS) * normw
            ob_ref[rows, hd * HEAD_DIM:(hd + 1) * HEAD_DIM] = (oe * zs_ref[hd, rows, :]).astype(BF16)
        return carry

    for c in range(nc):
        phase_b(c, 0)

    out = _dot(ob_ref[...], wout_ref[...])
    xn = x + m[2:3] * out
    y_ref[0] = xn * lax.rsqrt(jnp.mean(xn * xn, axis=-1, keepdims=True) + EPS) * fgain_ref[...]

    @pl.when(l == pl.num_programs(1) - 1)
    def _():
        cbo_ref[0] = cc_ref[...]
        so_ref[0] = s_ref[...]


def _gdn_layer(x, mod, cbuf, s0, gain, wqkv, wz, wab, convw, alog, dtb, normw, wout, fgain, *, tile):
    nb, seqlen, d = x.shape
    nv = s0.shape[1]
    nqk = nv // 2
    cdim = wqkv.shape[1]
    vdim = wz.shape[1]
    tile = tile if seqlen % tile == 0 else seqlen
    assert tile % CHUNK == 0
    nl = seqlen // tile
    nc = tile // CHUNK
    kern = functools.partial(_gdn_kernel, tile=tile)
    scratch = [
        pltpu.VMEM((CONV_CARRY, cdim), F32),
        pltpu.VMEM((nv, HEAD_DIM, HEAD_DIM), F32),
        pltpu.VMEM((nqk, tile, HEAD_DIM), F32),
        pltpu.VMEM((nqk, tile, HEAD_DIM), F32),
        pltpu.VMEM((nv, tile, HEAD_DIM), F32),
        pltpu.VMEM((nv, tile, HEAD_DIM), F32),
        pltpu.VMEM((tile, LANES), F32),
        pltpu.VMEM((tile, LANES), F32),
        pltpu.VMEM((nc * nqk, 2 * CHUNK), F32),
        pltpu.VMEM((nc * nqk, 2 * CHUNK), F32),
        pltpu.VMEM((nc * nqk, 2 * CHUNK), F32),
        pltpu.VMEM((nv, tile, HEAD_DIM), F32),
        pltpu.VMEM((nv, nc, 2 * CHUNK, HEAD_DIM), BF16),
        pltpu.VMEM((nqk, nc, 6 * CHUNK, 2 * CHUNK), BF16),
        pltpu.VMEM((nc * nv, LANES), F32),
        pltpu.VMEM((tile, vdim), BF16),
        pltpu.VMEM((CONV_CARRY + tile, 2 * HEAD_DIM), F32),
    ]
    return pl.pallas_call(
        kern,
        grid=(nb, nl),
        in_specs=[
            pl.BlockSpec((1, tile, d), lambda b, l: (b, l, 0)),
            pl.BlockSpec((1, 3, d), lambda b, l: (b, 0, 0)),
            pl.BlockSpec((1, CONV_CARRY, cdim), lambda b, l: (b, 0, 0)),
            pl.BlockSpec((1, nv, HEAD_DIM, HEAD_DIM), lambda b, l: (b, 0, 0, 0)),
            _const_spec((1, d)),
            _const_spec((d, cdim)),
            _const_spec((d, vdim)),
            _const_spec((d, LANES)),
            _const_spec((CONV_WIDTH, cdim)),
            _const_spec((1, LANES)),
            _const_spec((1, LANES)),
            _const_spec((1, HEAD_DIM)),
            _const_spec((vdim, d)),
            _const_spec((1, d)),
        ],
        out_specs=[
            pl.BlockSpec((1, tile, d), lambda b, l: (b, l, 0)),
            pl.BlockSpec((1, CONV_CARRY, cdim), lambda b, l: (b, 0, 0)),
            pl.BlockSpec((1, nv, HEAD_DIM, HEAD_DIM), lambda b, l: (b, 0, 0, 0)),
        ],
        out_shape=[
            jax.ShapeDtypeStruct((nb, seqlen, d), F32),
            jax.ShapeDtypeStruct((nb, CONV_CARRY, cdim), F32),
            jax.ShapeDtypeStruct((nb, nv, HEAD_DIM, HEAD_DIM), F32),
        ],
        scratch_shapes=scratch,
        compiler_params=pltpu.CompilerParams(
            dimension_semantics=("parallel", "arbitrary"), vmem_limit_bytes=VMEM_LIMIT_BYTES),
        name="gdn_layer",
    )(x, mod, cbuf, s0, gain, wqkv, wz, wab, convw, alog, dtb, normw, wout, fgain)


def _pair_order(nv):
    return jnp.concatenate([jnp.arange(0, nv, 2), jnp.arange(1, nv, 2)])


def _pad_lanes(row):
    return jnp.pad(row, (0, LANES - row.shape[0])).reshape(1, LANES)


def _trunk(x, mod, pool_buf, conv_buf, rec, start, weights, *, pool_tile, gdn_tile):
    (norm_gain, pool_w_in, pool_w_group, pool_scale, pool_w_out,
     wqkv, wz, wab, gdn_conv_w, alog, dtb, gdn_norm_w, gdn_w_out, final_gain) = weights
    d = x.shape[-1]
    buf = jnp.pad(pool_buf, ((0, 0), (POOL_CARRY - pool_buf.shape[1], 0), (0, 0)))
    x1, new_pool = _pool_layer(x, mod[0], buf, norm_gain[0].reshape(1, d), pool_w_in, pool_w_group,
                               pool_scale.reshape(1, -1), pool_w_out, start=start, tile=pool_tile)
    cbuf = jnp.pad(conv_buf, ((0, 0), (CONV_CARRY - conv_buf.shape[1], 0), (0, 0)))
    y, new_conv, new_rec = _gdn_layer(x1, mod[1], cbuf, rec, norm_gain[1].reshape(1, d), wqkv, wz, wab,
                                      gdn_conv_w, alog, dtb, gdn_norm_w.reshape(1, -1), gdn_w_out,
                                      final_gain.reshape(1, d), tile=gdn_tile)
    npool = pool_buf.shape[1]
    nconv = conv_buf.shape[1]
    return (y, new_pool[None, :, POOL_CARRY - npool:], new_conv[None, :, CONV_CARRY - nconv:], new_rec[None])


def kernel(x_prompt, x_sample, c_prompt, c_sample, state_pool, state_conv, state_rec, norm_gain, ada_w, ada_b,
           pool_w_in, pool_w_group, pool_scale, pool_w_out, gdn_w_in, gdn_conv_w, gdn_a_log, gdn_dt_bias,
           gdn_norm_w, gdn_w_out, final_gain):
    assert ada_w.shape[0] == 2 and pool_w_in.shape[0] == 1 and gdn_w_in.shape[0] == 1
    nbp = x_prompt.shape[0]
    nv = state_rec.shape[2]
    vdim = nv * HEAD_DIM
    cdim = state_conv.shape[-1]
    width = state_pool.shape[-1]

    mod = _adaln_mod(jnp.concatenate([c_prompt, c_sample], axis=0), ada_w, ada_b)
    mod = mod.transpose(0, 2, 1, 3)
    mod_p, mod_s = mod[:, :nbp], mod[:, nbp:]

    order = _pair_order(nv)
    w_in = gdn_w_in[0]
    wab = jnp.concatenate([w_in[:, cdim + vdim:cdim + vdim + nv][:, order],
                           w_in[:, cdim + vdim + nv:cdim + vdim + 2 * nv][:, order]], axis=1)
    wab = jnp.pad(wab, ((0, 0), (0, LANES - 2 * nv)))
    weights = (
        norm_gain,
        pool_w_in[0].astype(BF16), pool_w_group[0].astype(BF16), pool_scale[0], pool_w_out[0].astype(BF16),
        w_in[:, :cdim].astype(BF16), w_in[:, cdim:cdim + vdim].astype(BF16), wab,
        gdn_conv_w[0], _pad_lanes(gdn_a_log[0][order]), _pad_lanes(gdn_dt_bias[0][order]),
        gdn_norm_w[0], gdn_w_out[0].astype(BF16), final_gain,
    )

    zero_pool = jnp.zeros((nbp, POOL_CARRY - 1, width), F32)
    zero_conv = jnp.zeros((nbp, CONV_WIDTH - 1, cdim), F32)
    zero_rec = jnp.zeros((nbp, nv, HEAD_DIM, HEAD_DIM), F32)
    y_p, pool_p, conv_p, rec_p = _trunk(x_prompt, mod_p, zero_pool, zero_conv, zero_rec, 0, weights,
                                        pool_tile=512, gdn_tile=256)
    y_s, pool_s, conv_s, rec_s = _trunk(x_sample, mod_s, state_pool[0], state_conv[0], state_rec[0], PAST_LEN,
                                        weights, pool_tile=512, gdn_tile=256)
    return (y_p, y_s, pool_p, conv_p, rec_p, pool_s, conv_s, rec_s)
```

```python
import functools

import jax
import jax.numpy as jnp
from jax import lax
from jax.experimental import pallas as pl
from jax.experimental.pallas import tpu as pltpu

F32 = jnp.float32
BF16 = jnp.bfloat16
EPS = 1e-6

POOL_WINDOWS = (2, 4, 8, 16)
POOL_CARRY = 16
CONV_WIDTH = 4
CONV_CARRY = 8
HEAD_DIM = 128
CHUNK = 64
GROUP_CHUNKS = 2
LANES = 128
VMEM_LIMIT_BYTES = 58 * 1024 * 1024
PAST_LEN = 4096


def _dot(a, b):
    return jnp.dot(a, b, preferred_element_type=F32)


def _dot_nt(a, b):
    return lax.dot_general(a, b, (((1,), (1,)), ((), ())), preferred_element_type=F32)


def _split2(a):
    hi = a.astype(BF16)
    lo = (a - hi.astype(F32)).astype(BF16)
    return hi, lo


def _split3(a):
    p1 = a.astype(BF16)
    r1 = a - p1.astype(F32)
    p2 = r1.astype(BF16)
    p3 = (r1 - p2.astype(F32)).astype(BF16)
    return p1, p2, p3


def _mm3(a, b):
    ah, al = _split2(a)
    bh, bl = _split2(b)
    return _dot(ah, bh) + (_dot(ah, bl) + _dot(al, bh))


def _mm1(a, b):
    return _dot(a.astype(BF16), b.astype(BF16))


def _half_mask(r, c, s):
    return ((r // (2 * s)) == (c // (2 * s))) & ((r % (2 * s)) >= s) & ((c % (2 * s)) < s)


def _silu(x):
    return x / (1.0 + jnp.exp(-x))


def _softplus(x):
    return jnp.maximum(x, 0.0) + jnp.log(1.0 + jnp.exp(-jnp.abs(x)))


def _const_spec(shape):
    nd = len(shape)
    return pl.BlockSpec(shape, lambda *_: (0,) * nd, pipeline_mode=pl.Buffered(1))


def _mod_kernel(c_ref, w_ref, b_ref, o_ref):
    a = _silu(c_ref[...])
    o_ref[0, 0] = _mm3(a, w_ref[0]) + b_ref[0]


def _adaln_mod(c_all, ada_w, ada_b):
    depth, d, d3 = ada_w.shape
    nb = c_all.shape[0]
    nt = d3 // d
    return pl.pallas_call(
        _mod_kernel,
        grid=(depth, nt),
        in_specs=[
            pl.BlockSpec((nb, d), lambda i, n: (0, 0)),
            pl.BlockSpec((1, d, d), lambda i, n: (i, 0, n)),
            pl.BlockSpec((1, 1, d), lambda i, n: (i, 0, n)),
        ],
        out_specs=pl.BlockSpec((1, 1, nb, d), lambda i, n: (i, n, 0, 0)),
        out_shape=jax.ShapeDtypeStruct((depth, nt, nb, d), F32),
        name="adaln_mod",
    )(c_all, ada_w, ada_b.reshape(depth, 1, d3))


def _modulated_norm(x, m, gain):
    r = lax.rsqrt(jnp.mean(x * x, axis=-1, keepdims=True) + EPS)
    return x * r * gain * (1.0 + m[1:2]) + m[0:1]


def _pool_kernel(x_ref, mod_ref, buf_ref, gain_ref, win_ref, wg_ref, sc_ref, wout_ref,
                 xo_ref, bufo_ref, carry_ref, *, tile, start):
    l = pl.program_id(1)
    width = sc_ref.shape[1]
    gdim = width // len(POOL_WINDOWS)

    @pl.when(l == 0)
    def _():
        carry_ref[...] = buf_ref[0]

    x = x_ref[0]
    m = mod_ref[0]
    h = _modulated_norm(x, m, gain_ref[...]).astype(BF16)
    pos = (start + l * tile + lax.broadcasted_iota(jnp.int32, (tile, 1), 0)).astype(F32)

    acc = jnp.zeros(x.shape, F32)
    for gi, w in enumerate(POOL_WINDOWS):
        lo = gi * gdim
        u = _dot(h, win_ref[:, lo:lo + gdim])
        z = _dot(h, win_ref[:, width + lo:width + lo + gdim])
        ext = jnp.concatenate([carry_ref[:, lo:lo + gdim], u], axis=0)
        s = ext
        for k in range(gi + 1):
            s = s + pltpu.roll(s, 1 << k, 0)
        inv_cnt = 1.0 / jnp.minimum(pos + 1.0, float(w))
        d = s[POOL_CARRY:] * inv_cnt - u
        y = _dot(d.astype(BF16), wg_ref[gi])
        y = y * sc_ref[:, lo:lo + gdim] * _silu(z)
        acc = acc + _dot(y.astype(BF16), wout_ref[lo:lo + gdim, :])
        carry_ref[:, lo:lo + gdim] = ext[tile:tile + POOL_CARRY]

    xo_ref[0] = x + m[2:3] * acc

    @pl.when(l == pl.num_programs(1) - 1)
    def _():
        bufo_ref[0] = carry_ref[...]


def _pool_layer(x, mod, buf, gain, w_in, w_group, scale, w_out, *, start, tile):
    nb, seqlen, d = x.shape
    width = scale.shape[-1]
    ng, gdim, _ = w_group.shape
    tile = tile if seqlen % tile == 0 else seqlen
    nl = seqlen // tile
    kern = functools.partial(_pool_kernel, tile=tile, start=start)
    return pl.pallas_call(
        kern,
        grid=(nb, nl),
        in_specs=[
            pl.BlockSpec((1, tile, d), lambda b, l: (b, l, 0)),
            pl.BlockSpec((1, 3, d), lambda b, l: (b, 0, 0)),
            pl.BlockSpec((1, POOL_CARRY, width), lambda b, l: (b, 0, 0)),
            _const_spec((1, d)),
            _const_spec((d, 2 * width)),
            _const_spec((ng, gdim, gdim)),
            _const_spec((1, width)),
            _const_spec((width, d)),
        ],
        out_specs=[
            pl.BlockSpec((1, tile, d), lambda b, l: (b, l, 0)),
            pl.BlockSpec((1, POOL_CARRY, width), lambda b, l: (b, 0, 0)),
        ],
        out_shape=[
            jax.ShapeDtypeStruct((nb, seqlen, d), F32),
            jax.ShapeDtypeStruct((nb, POOL_CARRY, width), F32),
        ],
        scratch_shapes=[pltpu.VMEM((POOL_CARRY, width), F32)],
        compiler_params=pltpu.CompilerParams(
            dimension_semantics=("parallel", "arbitrary"), vmem_limit_bytes=VMEM_LIMIT_BYTES),
        name="pool_layer",
    )(x, mod, buf, gain, w_in, w_group, scale, w_out)


def _gdn_kernel(x_ref, mod_ref, cbuf_ref, s0_ref, gain_ref, wqkv_ref, wz_ref, wab_ref, convw_ref,
                alog_ref, dtb_ref, normw_ref, wout_ref, fgain_ref,
                y_ref, cbo_ref, so_ref,
                cc_ref, s_ref, qs_ref, ks_ref, vs_ref, zs_ref, gc_ref, bb_ref, gr_ref, br_ref, gl_ref,
                val_ref, kq_ref, akd_ref, eg_ref, ob_ref, ext_ref, *, tile):
    l = pl.program_id(1)
    nqk = qs_ref.shape[0]
    nv = vs_ref.shape[0]
    nc = tile // CHUNK
    pair = 2 * CHUNK

    @pl.when(l == 0)
    def _():
        cc_ref[...] = cbuf_ref[0]
        s_ref[...] = s0_ref[0]

    x = x_ref[0]
    m = mod_ref[0]
    hf = _modulated_norm(x, m, gain_ref[...])
    h, h_lo = _split2(hf)

    nblk = wqkv_ref.shape[1] // (2 * HEAD_DIM)

    def conv_block(blk):
        cs = slice(blk * 2 * HEAD_DIM, (blk + 1) * 2 * HEAD_DIM)
        pre = _dot(h, wqkv_ref[:, cs])
        ext_ref[0:CONV_CARRY, :] = cc_ref[:, cs]
        ext_ref[CONV_CARRY:, :] = pre
        cw = convw_ref[:, cs]
        acc = pre * cw[CONV_WIDTH - 1:CONV_WIDTH]
        for t in range(CONV_WIDTH - 1):
            acc = acc + ext_ref[pl.ds(CONV_CARRY - (CONV_WIDTH - 1) + t, tile), :] * cw[t:t + 1]
        conv = _silu(acc)
        cc_ref[:, cs] = ext_ref[pl.ds(tile, CONV_CARRY), :]
        for e in range(2):
            head = 2 * blk + e
            part = conv[:, e * HEAD_DIM:(e + 1) * HEAD_DIM]
            if head < 2 * nqk:
                inv = lax.rsqrt(jnp.sum(part * part, axis=-1, keepdims=True) + EPS)
                if head < nqk:
                    qs_ref[head] = part * (inv * (HEAD_DIM ** -0.5))
                else:
                    ks_ref[head - nqk] = part * inv
            else:
                vs_ref[head - 2 * nqk] = part

    for blk in range(nqk):
        conv_block(blk)
    v_blocks = list(range(nqk, nblk))

    def z_block(blk):
        zz = _silu(_dot(h, wz_ref[:, blk * 2 * HEAD_DIM:(blk + 1) * 2 * HEAD_DIM]))
        zs_ref[2 * blk] = zz[:, :HEAD_DIM]
        zs_ref[2 * blk + 1] = zz[:, HEAD_DIM:]

    z_blocks = list(range(nv // 2))
    assert len(z_blocks) == nqk

    wab_hi, wab_lo = _split2(wab_ref[...])
    ab = _dot(h, wab_hi) + (_dot(h, wab_lo) + _dot(h_lo, wab_hi))
    beta = 1.0 / (1.0 + jnp.exp(-ab))
    bb_ref[...] = beta
    g = -jnp.exp(alog_ref[...]) * _softplus(ab + dtb_ref[...])
    ri = lax.broadcasted_iota(jnp.int32, (tile, tile), 0)
    ci = lax.broadcasted_iota(jnp.int32, (tile, tile), 1)
    tril = ((ri >= ci) & ((ri // CHUNK) == (ci // CHUNK))).astype(BF16)
    g1, g2, g3 = _split3(g)
    gc = _dot(tril, g1) + (_dot(tril, g2) + _dot(tril, g3))
    gc_ref[...] = gc
    er = lax.broadcasted_iota(jnp.int32, (LANES, LANES), 0)
    ec = lax.broadcasted_iota(jnp.int32, (LANES, LANES), 1)
    eye = (er == ec).astype(BF16)

    def transpose_rows(a):
        a1, a2, a3 = _split3(a)
        return _dot_nt(eye, a1) + (_dot_nt(eye, a2) + _dot_nt(eye, a3))

    gct = transpose_rows(gc)
    bt = transpose_rows(beta)
    for c in range(nc):
        pr = slice(c * nqk, (c + 1) * nqk)
        cs = slice(c * CHUNK, (c + 1) * CHUNK)
        gr_ref[pr, 0:CHUNK] = gct[0:nqk, cs]
        gr_ref[pr, CHUNK:pair] = gct[nqk:nv, cs]
        br_ref[pr, 0:CHUNK] = bt[nv:nv + nqk, cs]
        br_ref[pr, CHUNK:pair] = bt[nv + nqk:2 * nv, cs]
        last = gct[0:nv, (c + 1) * CHUNK - 1:(c + 1) * CHUNK]
        gl_ref[pr, 0:CHUNK] = jnp.broadcast_to(last[0:nqk], (nqk, CHUNK))
        gl_ref[pr, CHUNK:pair] = jnp.broadcast_to(last[nqk:nv], (nqk, CHUNK))
        eg_ref[c * nv:(c + 1) * nv, :] = jnp.broadcast_to(jnp.exp(last), (nv, LANES))

    ii = lax.broadcasted_iota(jnp.int32, (CHUNK, pair), 0)
    lane = lax.broadcasted_iota(jnp.int32, (CHUNK, pair), 1)
    jj = lane % CHUNK
    left = lane < CHUNK
    incl = ii >= jj
    strict = ii > jj
    ident = (ii == jj).astype(F32)
    r2 = lax.broadcasted_iota(jnp.int32, (pair, pair), 0) % CHUNK
    c2 = lax.broadcasted_iota(jnp.int32, (pair, pair), 1) % CHUNK
    levels = []
    s = 2
    while s < CHUNK:
        levels.append(_half_mask(r2, c2, s))
        s *= 2
    base_mask = _half_mask(ii, jj, 1)
    pairs = range(nqk)

    zero_bd = jnp.zeros((pair, pair), BF16)

    def blockdiag(a):
        ab = a.astype(BF16)
        zero = jnp.zeros_like(ab)
        return jnp.concatenate([jnp.where(left, ab, zero), jnp.where(left, zero, ab)], axis=0)

    left2 = lax.broadcasted_iota(jnp.int32, (pair, pair), 1) < CHUNK

    def setup(c, j):
        rows = pl.ds(c * CHUNK, CHUNK)
        k = ks_ref[j, rows, :]
        q = qs_ref[j, rows, :]
        kst = jnp.concatenate([k, k], axis=0)
        qst = jnp.concatenate([q, q], axis=0)
        ga, gb = gc_ref[rows, j:j + 1], gc_ref[rows, nqk + j:nqk + j + 1]
        ba, bb = bb_ref[rows, nv + j:nv + j + 1], bb_ref[rows, nv + nqk + j:nv + nqk + j + 1]
        gcol = jnp.concatenate([ga, gb], axis=0)
        grow = gr_ref[pl.ds(c * nqk + j, 1), :]
        brow = br_ref[pl.ds(c * nqk + j, 1), :]
        dec = jnp.where(incl, jnp.exp(jnp.where(incl, jnp.where(left, ga, gb) - grow, 0.0)), 0.0)
        kb = kst.astype(BF16)
        kkqk = _dot_nt(jnp.concatenate([k, q], axis=0).astype(BF16), kb)
        kk, qk = kkqk[:CHUNK], kkqk[CHUNK:]
        lmat = jnp.where(strict, jnp.where(left, ba, bb) * kk * dec, 0.0)
        xinv = ident - jnp.where(base_mask, lmat, 0.0)
        kdt = (kst.T * jnp.exp(gl_ref[pl.ds(c * nqk + j, 1), :] - grow)).astype(BF16)
        zero = jnp.zeros_like(kdt)
        akd_ref[j, c, 0:pair, :] = blockdiag(qk * dec)
        akd_ref[j, c, pair:2 * pair, :] = jnp.where(left2, kdt, zero)
        akd_ref[j, c, 2 * pair:3 * pair, :] = jnp.where(left2, zero, kdt)
        return dict(kst=kst, qst=qst, gcol=gcol, grow=grow, brow=brow, lbd=blockdiag(lmat), xinv=xinv)

    def finish(c, j, ch, xinv):
        rows = pl.ds(c * CHUNK, CHUNK)
        xb = xinv * ch["brow"]
        vst = jnp.concatenate([vs_ref[2 * j, rows, :], vs_ref[2 * j + 1, rows, :]], axis=0)
        value = _dot(blockdiag(xb), vst.astype(BF16))
        kcd = _dot(blockdiag(xb * jnp.exp(ch["grow"])), ch["kst"].astype(BF16))
        qdec = ch["qst"] * jnp.exp(ch["gcol"])
        val_ref[2 * j, rows, :] = value[:CHUNK]
        val_ref[2 * j + 1, rows, :] = value[CHUNK:]
        kq_ref[2 * j, c] = jnp.concatenate([kcd[:CHUNK], qdec[:CHUNK]], axis=0).astype(BF16)
        kq_ref[2 * j + 1, c] = jnp.concatenate([kcd[CHUNK:], qdec[CHUNK:]], axis=0).astype(BF16)

    normw = normw_ref[...]
    heads = range(nv)

    def phase_b(c, carry):
        rows = pl.ds(c * CHUNK, CHUNK)
        states = [s_ref[hd] for hd in heads]
        r = [_dot(kq_ref[hd, c], states[hd].astype(BF16)) for hd in heads]
        vnew = [val_ref[hd, rows, :] - r[hd][:CHUNK] for hd in heads]
        vst = [jnp.concatenate([vnew[2 * j], vnew[2 * j + 1]], axis=0).astype(BF16) for j in pairs]
        au = [_dot(akd_ref[j, c], vst[j]) for j in pairs]
        for hd in heads:
            j, e = hd // 2, hd % 2
            upd = au[j][(1 + e) * pair:(2 + e) * pair]
            s_ref[hd] = states[hd] * eg_ref[pl.ds(c * nv + j + nqk * e, 1), :] + upd
            oe = r[hd][CHUNK:] + au[j][e * CHUNK:(e + 1) * CHUNK]
            oe = oe * lax.rsqrt(jnp.mean(oe * oe, axis=-1, keepdims=True) + EPS) * normw
            ob_ref[rows, hd * HEAD_DIM:(hd + 1) * HEAD_DIM] = (oe * zs_ref[hd, rows, :]).astype(BF16)
        return None

    groups = [[(c, j) for c in range(g, min(g + GROUP_CHUNKS, nc)) for j in pairs]
              for g in range(0, nc, GROUP_CHUNKS)]
    cur = []
    for n, (c, j) in enumerate(groups[0]):
        if n < len(z_blocks):
            z_block(z_blocks[n])
        cur.append(setup(c, j))
    for z in z_blocks[len(groups[0]):]:
        z_block(z)
    for g, chains in enumerate(groups):
        todo = [functools.partial(setup, c, j) for c, j in groups[g + 1]] if g + 1 < len(groups) else []
        if g == 0:
            todo = [functools.partial(conv_block, blk) for blk in v_blocks] + todo
        else:
            todo = [functools.partial(phase_b, c, 0) for c in sorted({c for c, _ in groups[g - 1]})] + todo
        per_slot = -(-len(todo) // (2 * len(levels)))
        nxt = []

        def fill(count):
            for _ in range(min(count, len(todo))):
                out = todo.pop(0)()
                if out is not None:
                    nxt.append(out)
        xinv = [ch["xinv"] for ch in cur]
        nch = range(len(chains))
        for mask in levels:
            t = [_dot(xinv[n].astype(BF16), jnp.where(mask, cur[n]["lbd"], zero_bd)) for n in nch]
            fill(per_slot)
            xinv = [xinv[n] - _dot(t[n].astype(BF16), blockdiag(xinv[n])) for n in nch]
            fill(per_slot)
        fill(len(todo))
        for n, (c, j) in enumerate(chains):
            finish(c, j, cur[n], xinv[n])
        cur = nxt
    for c in sorted({c for c, _ in groups[-1]}):
        phase_b(c, 0)

    out = _dot(ob_ref[...], wout_ref[...])
    xn = x + m[2:3] * out
    y_ref[0] = xn * lax.rsqrt(jnp.mean(xn * xn, axis=-1, keepdims=True) + EPS) * fgain_ref[...]

    @pl.when(l == pl.num_programs(1) - 1)
    def _():
        cbo_ref[0] = cc_ref[...]
        so_ref[0] = s_ref[...]


def _gdn_layer(x, mod, cbuf, s0, gain, wqkv, wz, wab, convw, alog, dtb, normw, wout, fgain, *, tile):
    nb, seqlen, d = x.shape
    nv = s0.shape[1]
    nqk = nv // 2
    cdim = wqkv.shape[1]
    vdim = wz.shape[1]
    tile = tile if seqlen % tile == 0 else seqlen
    assert tile % CHUNK == 0
    nl = seqlen // tile
    nc = tile // CHUNK
    kern = functools.partial(_gdn_kernel, tile=tile)
    scratch = [
        pltpu.VMEM((CONV_CARRY, cdim), F32),
        pltpu.VMEM((nv, HEAD_DIM, HEAD_DIM), F32),
        pltpu.VMEM((nqk, tile, HEAD_DIM), F32),
        pltpu.VMEM((nqk, tile, HEAD_DIM), F32),
        pltpu.VMEM((nv, tile, HEAD_DIM), F32),
        pltpu.VMEM((nv, tile, HEAD_DIM), F32),
        pltpu.VMEM((tile, LANES), F32),
        pltpu.VMEM((tile, LANES), F32),
        pltpu.VMEM((nc * nqk, 2 * CHUNK), F32),
        pltpu.VMEM((nc * nqk, 2 * CHUNK), F32),
        pltpu.VMEM((nc * nqk, 2 * CHUNK), F32),
        pltpu.VMEM((nv, tile, HEAD_DIM), F32),
        pltpu.VMEM((nv, nc, 2 * CHUNK, HEAD_DIM), BF16),
        pltpu.VMEM((nqk, nc, 6 * CHUNK, 2 * CHUNK), BF16),
        pltpu.VMEM((nc * nv, LANES), F32),
        pltpu.VMEM((tile, vdim), BF16),
        pltpu.VMEM((CONV_CARRY + tile, 2 * HEAD_DIM), F32),
    ]
    return pl.pallas_call(
        kern,
        grid=(nb, nl),
        in_specs=[
            pl.BlockSpec((1, tile, d), lambda b, l: (b, l, 0)),
            pl.BlockSpec((1, 3, d), lambda b, l: (b, 0, 0)),
            pl.BlockSpec((1, CONV_CARRY, cdim), lambda b, l: (b, 0, 0)),
            pl.BlockSpec((1, nv, HEAD_DIM, HEAD_DIM), lambda b, l: (b, 0, 0, 0)),
            _const_spec((1, d)),
            _const_spec((d, cdim)),
            _const_spec((d, vdim)),
            _const_spec((d, LANES)),
            _const_spec((CONV_WIDTH, cdim)),
            _const_spec((1, LANES)),
            _const_spec((1, LANES)),
            _const_spec((1, HEAD_DIM)),
            _const_spec((vdim, d)),
            _const_spec((1, d)),
        ],
        out_specs=[
            pl.BlockSpec((1, tile, d), lambda b, l: (b, l, 0)),
            pl.BlockSpec((1, CONV_CARRY, cdim), lambda b, l: (b, 0, 0)),
            pl.BlockSpec((1, nv, HEAD_DIM, HEAD_DIM), lambda b, l: (b, 0, 0, 0)),
        ],
        out_shape=[
            jax.ShapeDtypeStruct((nb, seqlen, d), F32),
            jax.ShapeDtypeStruct((nb, CONV_CARRY, cdim), F32),
            jax.ShapeDtypeStruct((nb, nv, HEAD_DIM, HEAD_DIM), F32),
        ],
        scratch_shapes=scratch,
        compiler_params=pltpu.CompilerParams(
            dimension_semantics=("parallel", "arbitrary"), vmem_limit_bytes=VMEM_LIMIT_BYTES),
        name="gdn_layer",
    )(x, mod, cbuf, s0, gain, wqkv, wz, wab, convw, alog, dtb, normw, wout, fgain)


def _pair_order(nv):
    return jnp.concatenate([jnp.arange(0, nv, 2), jnp.arange(1, nv, 2)])


def _pad_lanes(row):
    return jnp.pad(row, (0, LANES - row.shape[0])).reshape(1, LANES)


def _trunk(x, mod, pool_buf, conv_buf, rec, start, weights, *, pool_tile, gdn_tile):
    (norm_gain, pool_w_in, pool_w_group, pool_scale, pool_w_out,
     wqkv, wz, wab, gdn_conv_w, alog, dtb, gdn_norm_w, gdn_w_out, final_gain) = weights
    d = x.shape[-1]
    buf = jnp.pad(pool_buf, ((0, 0), (POOL_CARRY - pool_buf.shape[1], 0), (0, 0)))
    x1, new_pool = _pool_layer(x, mod[0], buf, norm_gain[0].reshape(1, d), pool_w_in, pool_w_group,
                               pool_scale.reshape(1, -1), pool_w_out, start=start, tile=pool_tile)
    cbuf = jnp.pad(conv_buf, ((0, 0), (CONV_CARRY - conv_buf.shape[1], 0), (0, 0)))
    y, new_conv, new_rec = _gdn_layer(x1, mod[1], cbuf, rec, norm_gain[1].reshape(1, d), wqkv, wz, wab,
                                      gdn_conv_w, alog, dtb, gdn_norm_w.reshape(1, -1), gdn_w_out,
                                      final_gain.reshape(1, d), tile=gdn_tile)
    npool = pool_buf.shape[1]
    nconv = conv_buf.shape[1]
    return (y, new_pool[None, :, POOL_CARRY - npool:], new_conv[None, :, CONV_CARRY - nconv:], new_rec[None])


def kernel(x_prompt, x_sample, c_prompt, c_sample, state_pool, state_conv, state_rec, norm_gain, ada_w, ada_b,
           pool_w_in, pool_w_group, pool_scale, pool_w_out, gdn_w_in, gdn_conv_w, gdn_a_log, gdn_dt_bias,
           gdn_norm_w, gdn_w_out, final_gain):
    assert ada_w.shape[0] == 2 and pool_w_in.shape[0] == 1 and gdn_w_in.shape[0] == 1
    nbp = x_prompt.shape[0]
    nv = state_rec.shape[2]
    vdim = nv * HEAD_DIM
    cdim = state_conv.shape[-1]
    width = state_pool.shape[-1]

    mod = _adaln_mod(jnp.concatenate([c_prompt, c_sample], axis=0), ada_w, ada_b)
    mod = mod.transpose(0, 2, 1, 3)
    mod_p, mod_s = mod[:, :nbp], mod[:, nbp:]

    order = _pair_order(nv)
    w_in = gdn_w_in[0]
    wab = jnp.concatenate([w_in[:, cdim + vdim:cdim + vdim + nv][:, order],
                           w_in[:, cdim + vdim + nv:cdim + vdim + 2 * nv][:, order]], axis=1)
    wab = jnp.pad(wab, ((0, 0), (0, LANES - 2 * nv)))
    weights = (
        norm_gain,
        pool_w_in[0].astype(BF16), pool_w_group[0].astype(BF16), pool_scale[0], pool_w_out[0].astype(BF16),
        w_in[:, :cdim].astype(BF16), w_in[:, cdim:cdim + vdim].astype(BF16), wab,
        gdn_conv_w[0], _pad_lanes(gdn_a_log[0][order]), _pad_lanes(gdn_dt_bias[0][order]),
        gdn_norm_w[0], gdn_w_out[0].astype(BF16), final_gain,
    )

    zero_pool = jnp.zeros((nbp, POOL_CARRY - 1, width), F32)
    zero_conv = jnp.zeros((nbp, CONV_WIDTH - 1, cdim), F32)
    zero_rec = jnp.zeros((nbp, nv, HEAD_DIM, HEAD_DIM), F32)
    y_p, pool_p, conv_p, rec_p = _trunk(x_prompt, mod_p, zero_pool, zero_conv, zero_rec, 0, weights,
                                        pool_tile=1024, gdn_tile=256)
    y_s, pool_s, conv_s, rec_s = _trunk(x_sample, mod_s, state_pool[0], state_conv[0], state_rec[0], PAST_LEN,
                                        weights, pool_tile=1024, gdn_tile=256)
    return (y_p, y_s, pool_p, conv_p, rec_p, pool_s, conv_s, rec_s)
```

```python
import functools

import jax
import jax.numpy as jnp
from jax import lax
from jax.experimental import pallas as pl
from jax.experimental.pallas import tpu as pltpu

F32 = jnp.float32
BF16 = jnp.bfloat16
EPS = 1e-6

POOL_WINDOWS = (2, 4, 8, 16)
POOL_CARRY = 16
CONV_WIDTH = 4
CONV_CARRY = 8
HEAD_DIM = 128
CHUNK = 64
GROUP_CHUNKS = 2
LANES = 128
VMEM_LIMIT_BYTES = 58 * 1024 * 1024
PAST_LEN = 4096


def _dot(a, b):
    return jnp.dot(a, b, preferred_element_type=F32)


def _dot_nt(a, b):
    return lax.dot_general(a, b, (((1,), (1,)), ((), ())), preferred_element_type=F32)


def _split2(a):
    hi = a.astype(BF16)
    lo = (a - hi.astype(F32)).astype(BF16)
    return hi, lo


def _split3(a):
    p1 = a.astype(BF16)
    r1 = a - p1.astype(F32)
    p2 = r1.astype(BF16)
    p3 = (r1 - p2.astype(F32)).astype(BF16)
    return p1, p2, p3


def _mm3(a, b):
    ah, al = _split2(a)
    bh, bl = _split2(b)
    return _dot(ah, bh) + (_dot(ah, bl) + _dot(al, bh))


def _mm1(a, b):
    return _dot(a.astype(BF16), b.astype(BF16))


def _half_mask(r, c, s):
    return ((r // (2 * s)) == (c // (2 * s))) & ((r % (2 * s)) >= s) & ((c % (2 * s)) < s)


def _silu(x):
    return x / (1.0 + jnp.exp(-x))


def _softplus(x):
    return jnp.maximum(x, 0.0) + jnp.log(1.0 + jnp.exp(-jnp.abs(x)))


def _const_spec(shape):
    nd = len(shape)
    return pl.BlockSpec(shape, lambda *_: (0,) * nd, pipeline_mode=pl.Buffered(1))


def _mod_kernel(c_ref, w_ref, b_ref, o_ref):
    a = _silu(c_ref[...])
    o_ref[0, 0] = _mm3(a, w_ref[0]) + b_ref[0]


def _adaln_mod(c_all, ada_w, ada_b):
    depth, d, d3 = ada_w.shape
    nb = c_all.shape[0]
    nt = d3 // d
    return pl.pallas_call(
        _mod_kernel,
        grid=(depth, nt),
        in_specs=[
            pl.BlockSpec((nb, d), lambda i, n: (0, 0)),
            pl.BlockSpec((1, d, d), lambda i, n: (i, 0, n)),
            pl.BlockSpec((1, 1, d), lambda i, n: (i, 0, n)),
        ],
        out_specs=pl.BlockSpec((1, 1, nb, d), lambda i, n: (i, n, 0, 0)),
        out_shape=jax.ShapeDtypeStruct((depth, nt, nb, d), F32),
        name="adaln_mod",
    )(c_all, ada_w, ada_b.reshape(depth, 1, d3))


def _modulated_norm(x, m, gain):
    r = lax.rsqrt(jnp.mean(x * x, axis=-1, keepdims=True) + EPS)
    return x * r * gain * (1.0 + m[1:2]) + m[0:1]


def _pool_kernel(x_ref, mod_ref, buf_ref, gain_ref, win_ref, wg_ref, sc_ref, wout_ref,
                 xo_ref, bufo_ref, carry_ref, *, tile, start):
    l = pl.program_id(1)
    width = sc_ref.shape[1]
    gdim = width // len(POOL_WINDOWS)

    @pl.when(l == 0)
    def _():
        carry_ref[...] = buf_ref[0]

    x = x_ref[0]
    m = mod_ref[0]
    h = _modulated_norm(x, m, gain_ref[...]).astype(BF16)
    pos = (start + l * tile + lax.broadcasted_iota(jnp.int32, (tile, 1), 0)).astype(F32)

    acc = jnp.zeros(x.shape, F32)
    for gi, w in enumerate(POOL_WINDOWS):
        lo = gi * gdim
        u = _dot(h, win_ref[:, lo:lo + gdim])
        z = _dot(h, win_ref[:, width + lo:width + lo + gdim])
        ext = jnp.concatenate([carry_ref[:, lo:lo + gdim], u], axis=0)
        s = ext
        for k in range(gi + 1):
            s = s + pltpu.roll(s, 1 << k, 0)
        inv_cnt = 1.0 / jnp.minimum(pos + 1.0, float(w))
        d = s[POOL_CARRY:] * inv_cnt - u
        y = _dot(d.astype(BF16), wg_ref[gi])
        y = y * sc_ref[:, lo:lo + gdim] * _silu(z)
        acc = acc + _dot(y.astype(BF16), wout_ref[lo:lo + gdim, :])
        carry_ref[:, lo:lo + gdim] = ext[tile:tile + POOL_CARRY]

    xo_ref[0] = x + m[2:3] * acc

    @pl.when(l == pl.num_programs(1) - 1)
    def _():
        bufo_ref[0] = carry_ref[...]


def _pool_layer(x, mod, buf, gain, w_in, w_group, scale, w_out, *, start, tile):
    nb, seqlen, d = x.shape
    width = scale.shape[-1]
    ng, gdim, _ = w_group.shape
    tile = tile if seqlen % tile == 0 else seqlen
    nl = seqlen // tile
    kern = functools.partial(_pool_kernel, tile=tile, start=start)
    return pl.pallas_call(
        kern,
        grid=(nb, nl),
        in_specs=[
            pl.BlockSpec((1, tile, d), lambda b, l: (b, l, 0)),
            pl.BlockSpec((1, 3, d), lambda b, l: (b, 0, 0)),
            pl.BlockSpec((1, POOL_CARRY, width), lambda b, l: (b, 0, 0)),
            _const_spec((1, d)),
            _const_spec((d, 2 * width)),
            _const_spec((ng, gdim, gdim)),
            _const_spec((1, width)),
            _const_spec((width, d)),
        ],
        out_specs=[
            pl.BlockSpec((1, tile, d), lambda b, l: (b, l, 0)),
            pl.BlockSpec((1, POOL_CARRY, width), lambda b, l: (b, 0, 0)),
        ],
        out_shape=[
            jax.ShapeDtypeStruct((nb, seqlen, d), F32),
            jax.ShapeDtypeStruct((nb, POOL_CARRY, width), F32),
        ],
        scratch_shapes=[pltpu.VMEM((POOL_CARRY, width), F32)],
        compiler_params=pltpu.CompilerParams(
            dimension_semantics=("parallel", "arbitrary"), vmem_limit_bytes=VMEM_LIMIT_BYTES),
        name="pool_layer",
    )(x, mod, buf, gain, w_in, w_group, scale, w_out)


def _gdn_kernel(x_ref, mod_ref, cbuf_ref, s0_ref, gain_ref, wqkv_ref, wz_ref, wab_ref, convw_ref,
                alog_ref, dtb_ref, normw_ref, wout_ref, fgain_ref,
                y_ref, cbo_ref, so_ref,
                cc_ref, s_ref, qs_ref, ks_ref, vs_ref, zs_ref, gc_ref, bb_ref, gr_ref, br_ref, gl_ref,
                val_ref, kq_ref, akd_ref, eg_ref, ob_ref, ext_ref, *, tile):
    l = pl.program_id(1)
    nqk = qs_ref.shape[0]
    nv = vs_ref.shape[0]
    nc = tile // CHUNK
    pair = 2 * CHUNK

    @pl.when(l == 0)
    def _():
        cc_ref[...] = cbuf_ref[0]
        s_ref[...] = s0_ref[0]

    x = x_ref[0]
    m = mod_ref[0]
    hf = _modulated_norm(x, m, gain_ref[...])
    h, h_lo = _split2(hf)

    nblk = wqkv_ref.shape[1] // (2 * HEAD_DIM)

    def conv_block(blk):
        cs = slice(blk * 2 * HEAD_DIM, (blk + 1) * 2 * HEAD_DIM)
        pre = _dot(h, wqkv_ref[:, cs])
        ext_ref[0:CONV_CARRY, :] = cc_ref[:, cs]
        ext_ref[CONV_CARRY:, :] = pre
        cw = convw_ref[:, cs]
        acc = pre * cw[CONV_WIDTH - 1:CONV_WIDTH]
        for t in range(CONV_WIDTH - 1):
            acc = acc + ext_ref[pl.ds(CONV_CARRY - (CONV_WIDTH - 1) + t, tile), :] * cw[t:t + 1]
        conv = _silu(acc)
        cc_ref[:, cs] = ext_ref[pl.ds(tile, CONV_CARRY), :]
        for e in range(2):
            head = 2 * blk + e
            part = conv[:, e * HEAD_DIM:(e + 1) * HEAD_DIM]
            if head < 2 * nqk:
                inv = lax.rsqrt(jnp.sum(part * part, axis=-1, keepdims=True) + EPS)
                if head < nqk:
                    qs_ref[head] = part * (inv * (HEAD_DIM ** -0.5))
                else:
                    ks_ref[head - nqk] = part * inv
            else:
                vs_ref[head - 2 * nqk] = part

    for blk in range(nqk):
        conv_block(blk)
    v_blocks = list(range(nqk, nblk))

    def z_block(blk):
        zz = _silu(_dot(h, wz_ref[:, blk * 2 * HEAD_DIM:(blk + 1) * 2 * HEAD_DIM]))
        zs_ref[2 * blk] = zz[:, :HEAD_DIM]
        zs_ref[2 * blk + 1] = zz[:, HEAD_DIM:]

    z_blocks = list(range(nv // 2))
    assert len(z_blocks) == nqk

    wab_hi, wab_lo = _split2(wab_ref[...])
    ab = _dot(h, wab_hi) + (_dot(h, wab_lo) + _dot(h_lo, wab_hi))
    beta = 1.0 / (1.0 + jnp.exp(-ab))
    bb_ref[...] = beta
    g = -jnp.exp(alog_ref[...]) * _softplus(ab + dtb_ref[...])
    ri = lax.broadcasted_iota(jnp.int32, (tile, tile), 0)
    ci = lax.broadcasted_iota(jnp.int32, (tile, tile), 1)
    tril = ((ri >= ci) & ((ri // CHUNK) == (ci // CHUNK))).astype(BF16)
    g1, g2, g3 = _split3(g)
    gc = _dot(tril, g1) + (_dot(tril, g2) + _dot(tril, g3))
    gc_ref[...] = gc
    er = lax.broadcasted_iota(jnp.int32, (LANES, LANES), 0)
    ec = lax.broadcasted_iota(jnp.int32, (LANES, LANES), 1)
    rr = er % nv
    src = (er // nv) * nv + jnp.where(rr < nqk, 2 * rr, 2 * (rr - nqk) + 1)
    eye = (ec == jnp.where(er < 2 * nv, src, er)).astype(BF16)

    def transpose_rows(a):
        a1, a2, a3 = _split3(a)
        return _dot_nt(eye, a1) + (_dot_nt(eye, a2) + _dot_nt(eye, a3))

    gct = transpose_rows(gc)
    bt = transpose_rows(beta)
    for c in range(nc):
        pr = slice(c * nqk, (c + 1) * nqk)
        cs = slice(c * CHUNK, (c + 1) * CHUNK)
        gr_ref[pr, 0:CHUNK] = gct[0:nqk, cs]
        gr_ref[pr, CHUNK:pair] = gct[nqk:nv, cs]
        br_ref[pr, 0:CHUNK] = bt[nv:nv + nqk, cs]
        br_ref[pr, CHUNK:pair] = bt[nv + nqk:2 * nv, cs]
        last = gct[0:nv, (c + 1) * CHUNK - 1:(c + 1) * CHUNK]
        gl_ref[pr, 0:CHUNK] = jnp.broadcast_to(last[0:nqk], (nqk, CHUNK))
        gl_ref[pr, CHUNK:pair] = jnp.broadcast_to(last[nqk:nv], (nqk, CHUNK))
        eg_ref[c * nv:(c + 1) * nv, :] = jnp.broadcast_to(jnp.exp(last), (nv, LANES))

    ii = lax.broadcasted_iota(jnp.int32, (CHUNK, pair), 0)
    lane = lax.broadcasted_iota(jnp.int32, (CHUNK, pair), 1)
    jj = lane % CHUNK
    left = lane < CHUNK
    incl = ii >= jj
    strict = ii > jj
    ident = (ii == jj).astype(F32)
    r2 = lax.broadcasted_iota(jnp.int32, (pair, pair), 0) % CHUNK
    c2 = lax.broadcasted_iota(jnp.int32, (pair, pair), 1) % CHUNK
    levels = []
    s = 2
    while s < CHUNK:
        levels.append(_half_mask(r2, c2, s))
        s *= 2
    base_mask = _half_mask(ii, jj, 1)
    pairs = range(nqk)

    zero_bd = jnp.zeros((pair, pair), BF16)

    def blockdiag(a):
        ab = a.astype(BF16)
        zero = jnp.zeros_like(ab)
        return jnp.concatenate([jnp.where(left, ab, zero), jnp.where(left, zero, ab)], axis=0)

    left2 = lax.broadcasted_iota(jnp.int32, (pair, pair), 1) < CHUNK

    def setup(c, j):
        rows = pl.ds(c * CHUNK, CHUNK)
        k = ks_ref[j, rows, :]
        q = qs_ref[j, rows, :]
        kst = jnp.concatenate([k, k], axis=0)
        qst = jnp.concatenate([q, q], axis=0)
        ga, gb = gc_ref[rows, 2 * j:2 * j + 1], gc_ref[rows, 2 * j + 1:2 * j + 2]
        ba, bb = bb_ref[rows, nv + 2 * j:nv + 2 * j + 1], bb_ref[rows, nv + 2 * j + 1:nv + 2 * j + 2]
        gcol = jnp.concatenate([ga, gb], axis=0)
        grow = gr_ref[pl.ds(c * nqk + j, 1), :]
        brow = br_ref[pl.ds(c * nqk + j, 1), :]
        dec = jnp.where(incl, jnp.exp(jnp.where(incl, jnp.where(left, ga, gb) - grow, 0.0)), 0.0)
        kb = kst.astype(BF16)
        kkqk = _dot_nt(jnp.concatenate([k, q], axis=0).astype(BF16), kb)
        kk, qk = kkqk[:CHUNK], kkqk[CHUNK:]
        lmat = jnp.where(strict, jnp.where(left, ba, bb) * kk * dec, 0.0)
        xinv = ident - jnp.where(base_mask, lmat, 0.0)
        kdt = (kst.T * jnp.exp(gl_ref[pl.ds(c * nqk + j, 1), :] - grow)).astype(BF16)
        zero = jnp.zeros_like(kdt)
        akd_ref[j, c, 0:pair, :] = blockdiag(qk * dec)
        akd_ref[j, c, pair:2 * pair, :] = jnp.where(left2, kdt, zero)
        akd_ref[j, c, 2 * pair:3 * pair, :] = jnp.where(left2, zero, kdt)
        return dict(kst=kst, qst=qst, gcol=gcol, grow=grow, brow=brow, lbd=blockdiag(lmat), xinv=xinv)

    def finish(c, j, ch, xinv):
        rows = pl.ds(c * CHUNK, CHUNK)
        xb = xinv * ch["brow"]
        vst = jnp.concatenate([vs_ref[2 * j, rows, :], vs_ref[2 * j + 1, rows, :]], axis=0)
        value = _dot(blockdiag(xb), vst.astype(BF16))
        kcd = _dot(blockdiag(xb * jnp.exp(ch["grow"])), ch["kst"].astype(BF16))
        qdec = ch["qst"] * jnp.exp(ch["gcol"])
        val_ref[2 * j, rows, :] = value[:CHUNK]
        val_ref[2 * j + 1, rows, :] = value[CHUNK:]
        kq_ref[2 * j, c] = jnp.concatenate([kcd[:CHUNK], qdec[:CHUNK]], axis=0).astype(BF16)
        kq_ref[2 * j + 1, c] = jnp.concatenate([kcd[CHUNK:], qdec[CHUNK:]], axis=0).astype(BF16)

    normw = normw_ref[...]
    heads = range(nv)

    def phase_b(c, carry):
        rows = pl.ds(c * CHUNK, CHUNK)
        states = [s_ref[hd] for hd in heads]
        r = [_dot(kq_ref[hd, c], states[hd].astype(BF16)) for hd in heads]
        vnew = [val_ref[hd, rows, :] - r[hd][:CHUNK] for hd in heads]
        vst = [jnp.concatenate([vnew[2 * j], vnew[2 * j + 1]], axis=0).astype(BF16) for j in pairs]
        au = [_dot(akd_ref[j, c], vst[j]) for j in pairs]
        for hd in heads:
            j, e = hd // 2, hd % 2
            upd = au[j][(1 + e) * pair:(2 + e) * pair]
            s_ref[hd] = states[hd] * eg_ref[pl.ds(c * nv + j + nqk * e, 1), :] + upd
            oe = r[hd][CHUNK:] + au[j][e * CHUNK:(e + 1) * CHUNK]
            oe = oe * lax.rsqrt(jnp.mean(oe * oe, axis=-1, keepdims=True) + EPS) * normw
            ob_ref[rows, hd * HEAD_DIM:(hd + 1) * HEAD_DIM] = (oe * zs_ref[hd, rows, :]).astype(BF16)
        return None

    groups = [[(c, j) for c in range(g, min(g + GROUP_CHUNKS, nc)) for j in pairs]
              for g in range(0, nc, GROUP_CHUNKS)]
    cur = []
    for n, (c, j) in enumerate(groups[0]):
        if n < len(z_blocks):
            z_block(z_blocks[n])
        cur.append(setup(c, j))
    for z in z_blocks[len(groups[0]):]:
        z_block(z)
    for g, chains in enumerate(groups):
        todo = [functools.partial(setup, c, j) for c, j in groups[g + 1]] if g + 1 < len(groups) else []
        if g == 0:
            todo = [functools.partial(conv_block, blk) for blk in v_blocks] + todo
        else:
            todo = [functools.partial(phase_b, c, 0) for c in sorted({c for c, _ in groups[g - 1]})] + todo
        per_slot = -(-len(todo) // (2 * len(levels)))
        nxt = []

        def fill(count):
            for _ in range(min(count, len(todo))):
                out = todo.pop(0)()
                if out is not None:
                    nxt.append(out)
        xinv = [ch["xinv"] for ch in cur]
        nch = range(len(chains))
        for mask in levels:
            t = [_dot(xinv[n].astype(BF16), jnp.where(mask, cur[n]["lbd"], zero_bd)) for n in nch]
            fill(per_slot)
            xinv = [xinv[n] - _dot(t[n].astype(BF16), blockdiag(xinv[n])) for n in nch]
            fill(per_slot)
        fill(len(todo))
        for n, (c, j) in enumerate(chains):
            finish(c, j, cur[n], xinv[n])
        cur = nxt
    for c in sorted({c for c, _ in groups[-1]}):
        phase_b(c, 0)

    out = _dot(ob_ref[...], wout_ref[...])
    xn = x + m[2:3] * out
    y_ref[0] = xn * lax.rsqrt(jnp.mean(xn * xn, axis=-1, keepdims=True) + EPS) * fgain_ref[...]

    @pl.when(l == pl.num_programs(1) - 1)
    def _():
        cbo_ref[0] = cc_ref[...]
        so_ref[0] = s_ref[...]


def _gdn_layer(x, mod, cbuf, s0, gain, wqkv, wz, wab, convw, alog, dtb, normw, wout, fgain, *, tile):
    nb, seqlen, d = x.shape
    nv = s0.shape[1]
    nqk = nv // 2
    cdim = wqkv.shape[1]
    vdim = wz.shape[1]
    tile = tile if seqlen % tile == 0 else seqlen
    assert tile % CHUNK == 0
    nl = seqlen // tile
    nc = tile // CHUNK
    kern = functools.partial(_gdn_kernel, tile=tile)
    scratch = [
        pltpu.VMEM((CONV_CARRY, cdim), F32),
        pltpu.VMEM((nv, HEAD_DIM, HEAD_DIM), F32),
        pltpu.VMEM((nqk, tile, HEAD_DIM), F32),
        pltpu.VMEM((nqk, tile, HEAD_DIM), F32),
        pltpu.VMEM((nv, tile, HEAD_DIM), F32),
        pltpu.VMEM((nv, tile, HEAD_DIM), F32),
        pltpu.VMEM((tile, LANES), F32),
        pltpu.VMEM((tile, LANES), F32),
        pltpu.VMEM((nc * nqk, 2 * CHUNK), F32),
        pltpu.VMEM((nc * nqk, 2 * CHUNK), F32),
        pltpu.VMEM((nc * nqk, 2 * CHUNK), F32),
        pltpu.VMEM((nv, tile, HEAD_DIM), F32),
        pltpu.VMEM((nv, nc, 2 * CHUNK, HEAD_DIM), BF16),
        pltpu.VMEM((nqk, nc, 6 * CHUNK, 2 * CHUNK), BF16),
        pltpu.VMEM((nc * nv, LANES), F32),
        pltpu.VMEM((tile, vdim), BF16),
        pltpu.VMEM((CONV_CARRY + tile, 2 * HEAD_DIM), F32),
    ]
    return pl.pallas_call(
        kern,
        grid=(nb, nl),
        in_specs=[
            pl.BlockSpec((1, tile, d), lambda b, l: (b, l, 0)),
            pl.BlockSpec((1, 3, d), lambda b, l: (b, 0, 0)),
            pl.BlockSpec((1, CONV_CARRY, cdim), lambda b, l: (b, 0, 0)),
            pl.BlockSpec((1, nv, HEAD_DIM, HEAD_DIM), lambda b, l: (b, 0, 0, 0)),
            _const_spec((1, d)),
            _const_spec((d, cdim)),
            _const_spec((d, vdim)),
            _const_spec((d, LANES)),
            _const_spec((CONV_WIDTH, cdim)),
            _const_spec((1, LANES)),
            _const_spec((1, LANES)),
            _const_spec((1, HEAD_DIM)),
            _const_spec((vdim, d)),
            _const_spec((1, d)),
        ],
        out_specs=[
            pl.BlockSpec((1, tile, d), lambda b, l: (b, l, 0)),
            pl.BlockSpec((1, CONV_CARRY, cdim), lambda b, l: (b, 0, 0)),
            pl.BlockSpec((1, nv, HEAD_DIM, HEAD_DIM), lambda b, l: (b, 0, 0, 0)),
        ],
        out_shape=[
            jax.ShapeDtypeStruct((nb, seqlen, d), F32),
            jax.ShapeDtypeStruct((nb, CONV_CARRY, cdim), F32),
            jax.ShapeDtypeStruct((nb, nv, HEAD_DIM, HEAD_DIM), F32),
        ],
        scratch_shapes=scratch,
        compiler_params=pltpu.CompilerParams(
            dimension_semantics=("parallel", "arbitrary"), vmem_limit_bytes=VMEM_LIMIT_BYTES),
        name="gdn_layer",
    )(x, mod, cbuf, s0, gain, wqkv, wz, wab, convw, alog, dtb, normw, wout, fgain)


def _pad_lanes(row):
    return jnp.pad(row, (0, LANES - row.shape[0])).reshape(1, LANES)


def _trunk(x, mod, pool_buf, conv_buf, rec, start, weights, *, pool_tile, gdn_tile):
    (norm_gain, pool_w_in, pool_w_group, pool_scale, pool_w_out,
     wqkv, wz, wab, gdn_conv_w, alog, dtb, gdn_norm_w, gdn_w_out, final_gain) = weights
    d = x.shape[-1]
    buf = jnp.pad(pool_buf, ((0, 0), (POOL_CARRY - pool_buf.shape[1], 0), (0, 0)))
    x1, new_pool = _pool_layer(x, mod[0], buf, norm_gain[0].reshape(1, d), pool_w_in, pool_w_group,
                               pool_scale.reshape(1, -1), pool_w_out, start=start, tile=pool_tile)
    cbuf = jnp.pad(conv_buf, ((0, 0), (CONV_CARRY - conv_buf.shape[1], 0), (0, 0)))
    y, new_conv, new_rec = _gdn_layer(x1, mod[1], cbuf, rec, norm_gain[1].reshape(1, d), wqkv, wz, wab,
                                      gdn_conv_w, alog, dtb, gdn_norm_w.reshape(1, -1), gdn_w_out,
                                      final_gain.reshape(1, d), tile=gdn_tile)
    npool = pool_buf.shape[1]
    nconv = conv_buf.shape[1]
    return (y, new_pool[None, :, POOL_CARRY - npool:], new_conv[None, :, CONV_CARRY - nconv:], new_rec[None])


def kernel(x_prompt, x_sample, c_prompt, c_sample, state_pool, state_conv, state_rec, norm_gain, ada_w, ada_b,
           pool_w_in, pool_w_group, pool_scale, pool_w_out, gdn_w_in, gdn_conv_w, gdn_a_log, gdn_dt_bias,
           gdn_norm_w, gdn_w_out, final_gain):
    assert ada_w.shape[0] == 2 and pool_w_in.shape[0] == 1 and gdn_w_in.shape[0] == 1
    nbp = x_prompt.shape[0]
    nv = state_rec.shape[2]
    vdim = nv * HEAD_DIM
    cdim = state_conv.shape[-1]
    width = state_pool.shape[-1]

    mod = _adaln_mod(jnp.concatenate([c_prompt, c_sample], axis=0), ada_w, ada_b)
    mod = mod.transpose(0, 2, 1, 3)
    mod_p, mod_s = mod[:, :nbp], mod[:, nbp:]

    w_in = gdn_w_in[0]
    wab = jnp.pad(w_in[:, cdim + vdim:cdim + vdim + 2 * nv], ((0, 0), (0, LANES - 2 * nv)))
    weights = (
        norm_gain,
        pool_w_in[0].astype(BF16), pool_w_group[0].astype(BF16), pool_scale[0], pool_w_out[0].astype(BF16),
        w_in[:, :cdim].astype(BF16), w_in[:, cdim:cdim + vdim].astype(BF16), wab,
        gdn_conv_w[0], _pad_lanes(gdn_a_log[0]), _pad_lanes(gdn_dt_bias[0]),
        gdn_norm_w[0], gdn_w_out[0].astype(BF16), final_gain,
    )

    zero_pool = jnp.zeros((nbp, POOL_CARRY - 1, width), F32)
    zero_conv = jnp.zeros((nbp, CONV_WIDTH - 1, cdim), F32)
    zero_rec = jnp.zeros((nbp, nv, HEAD_DIM, HEAD_DIM), F32)
    y_p, pool_p, conv_p, rec_p = _trunk(x_prompt, mod_p, zero_pool, zero_conv, zero_rec, 0, weights,
                                        pool_tile=1024, gdn_tile=256)
    y_s, pool_s, conv_s, rec_s = _trunk(x_sample, mod_s, state_pool[0], state_conv[0], state_rec[0], PAST_LEN,
                                        weights, pool_tile=1024, gdn_tile=256)
    return (y_p, y_s, pool_p, conv_p, rec_p, pool_s, conv_s, rec_s)
```
